```python
import math
import numpy as np
import jax
import jax.numpy as jnp
from jax import lax

D_MODEL = 1024
BATCH = 8
SEQ = 2048
DEPTH = 4
DEC_BATCH = 128
DEC_SEQ = 8
PAST_LEN = 2048
PAGE_SIZE = 128

MIX_W = D_MODEL // 2
HG_DK = 128
HG_HEADS = MIX_W // HG_DK
HG_DV = MIX_W // HG_HEADS
HG_CHUNK = 64
NSA_HD = 64
NSA_HEADS = MIX_W // NSA_HD
NSA_KV_HEADS = 2
NSA_GROUP = NSA_HEADS // NSA_KV_HEADS
KV_W = NSA_KV_HEADS * NSA_HD
CMP_BLOCK = 32
CMP_STRIDE = 16
SLC_BLOCK = 64
TOP_N = 16
WINDOW = 512
WIN_QBLOCK = 128
SLC_QBLOCK = 64
ROPE_THETA = 500000.0
ROPE_DIM = NSA_HD // 4
D_FF = ((8 * D_MODEL // 3) + 127) // 128 * 128
CONV_W = 3
N_BRANCH = 2
NORM_EPS = 1e-6
NEG_INF = -1e30
LB_FLOOR = 1e-30
FORCE_SCORE = 1e9
IN_SIZES = (MIX_W, MIX_W, MIX_W, MIX_W, MIX_W, KV_W, KV_W, KV_W, KV_W, KV_W, KV_W, 3 * NSA_HEADS, N_BRANCH * D_MODEL)

kernel_name = 'hgrn2_nsa_convffn_hybrid_step'


def rmsnorm(x, g):
    x32 = x.astype(jnp.float32)
    y = x32 * lax.rsqrt(jnp.mean(x32 * x32, axis=-1, keepdims=True) + NORM_EPS)
    return (y * g.astype(jnp.float32)).astype(x.dtype)


def rope_partial(x, pos):
    half = ROPE_DIM // 2
    inv = jnp.float32(ROPE_THETA) ** (-jnp.arange(half, dtype=jnp.float32) * 2.0 / ROPE_DIM)
    ang = pos.astype(jnp.float32)[:, None] * inv[None, :]
    cos = jnp.cos(ang)[:, None, :]
    sin = jnp.sin(ang)[:, None, :]
    x32 = x.astype(jnp.float32)
    x1 = x32[..., :half]
    x2 = x32[..., half:ROPE_DIM]
    out = jnp.concatenate([x1 * cos - x2 * sin, x1 * sin + x2 * cos, x32[..., ROPE_DIM:]], axis=-1)
    return out.astype(x.dtype)


def masked_softmax(s, mask):
    s = jnp.where(mask, s.astype(jnp.float32), NEG_INF)
    m = jnp.max(s, axis=-1, keepdims=True)
    e = jnp.where(mask, jnp.exp(s - m), 0.0)
    return e / jnp.maximum(jnp.sum(e, axis=-1, keepdims=True), 1e-30)


def gla_chunked(q, k, v, log_f, s0, chunk):
    B, L, H, DK = q.shape
    DV = v.shape[-1]
    n = L // chunk

    def to_chunks(a):
        return a.reshape(B, n, chunk, H, a.shape[-1]).transpose(1, 0, 3, 2, 4)

    causal = jnp.tril(jnp.ones((chunk, chunk), dtype=bool))[None, None, :, :, None]

    def step(S, inp):
        qc, kc, vc, gc = inp
        b = jnp.cumsum(gc, axis=2)
        inter = jnp.einsum('bhtd,bhde->bhte', qc * jnp.exp(b), S)
        diff = b[:, :, :, None, :] - b[:, :, None, :, :]
        decay = jnp.exp(jnp.where(causal, diff, NEG_INF))
        attn = jnp.einsum('bhtd,bhsd,bhtsd->bhts', qc, kc, decay)
        intra = jnp.einsum('bhts,bhse->bhte', attn, vc)
        b_last = b[:, :, -1, :]
        S_new = jnp.exp(b_last)[..., None] * S + jnp.einsum('bhsd,bhse->bhde', kc * jnp.exp(b_last[:, :, None, :] - b), vc)
        return S_new, inter + intra

    S, o = lax.scan(step, s0, (to_chunks(q), to_chunks(k), to_chunks(v), to_chunks(log_f)))
    return o.transpose(1, 0, 3, 2, 4).reshape(B, L, H, DV), S


def hgrn2_mixer(q_raw, f_raw, i_raw, g_raw, lb, norm_g, s0):
    B, L = q_raw.shape[:2]
    q = jax.nn.silu(q_raw.astype(jnp.float32)) * (HG_DK ** -0.5)
    fr = f_raw.astype(jnp.float32)
    log_f = jnp.logaddexp(jnp.log(jnp.maximum(lb, LB_FLOOR)), jnp.log1p(-lb) + jax.nn.log_sigmoid(fr))
    k = (1.0 - lb) * jax.nn.sigmoid(-fr)
    v = i_raw.astype(jnp.float32)
    o, s_final = gla_chunked(q, k, v, log_f, s0, math.gcd(HG_CHUNK, L))
    o = rmsnorm(o, norm_g) * jax.nn.silu(g_raw.astype(jnp.float32))
    return o.reshape(B, L, HG_HEADS * HG_DV).astype(q_raw.dtype), s_final


def compress_blocks(rows, pe, w1, w2):
    T = rows.shape[1]
    n_cmp = (T - CMP_BLOCK) // CMP_STRIDE + 1
    idx = (jnp.arange(n_cmp) * CMP_STRIDE)[:, None] + jnp.arange(CMP_BLOCK)[None, :]
    blk = rows[:, idx] + pe[None, None, :, None, :].astype(rows.dtype)
    h = jax.nn.gelu(jnp.einsum('bnjgd,jde->bnge', blk, w1))
    return jnp.einsum('bnge,ef->bngf', h, w2)


def nsa_cmp_slc(q, pos, cmp_rows, slc_rows, cmp_pe, cmp_w1, cmp_w2):
    B, L = q.shape[:2]
    T = cmp_rows.shape[1]
    scale = NSA_HD ** -0.5
    qg = q.reshape(B, L, NSA_KV_HEADS, NSA_GROUP, NSA_HD)
    k_cmp = compress_blocks(cmp_rows[:, :, 0], cmp_pe[0], cmp_w1[0], cmp_w2[0])
    v_cmp = compress_blocks(cmp_rows[:, :, 1], cmp_pe[1], cmp_w1[1], cmp_w2[1])
    n_cmp = k_cmp.shape[1]
    c_start = jnp.arange(n_cmp, dtype=jnp.int32) * CMP_STRIDE
    c_end = c_start + CMP_BLOCK - 1
    k_cmp = rope_partial(k_cmp, c_end)
    s = jnp.einsum('blgrd,bngd->blgrn', qg, k_cmp) * scale
    cmask = (c_end[None, :] <= pos[:, None])[None, :, None, None, :]
    p_cmp = masked_softmax(s, cmask)
    o_cmp = jnp.einsum('blgrn,bngd->blgrd', p_cmp.astype(v_cmp.dtype), v_cmp).reshape(B, L, NSA_HEADS, NSA_HD)
    n_slc = -(-T // SLC_BLOCK)
    s_start = jnp.arange(n_slc, dtype=jnp.int32) * SLC_BLOCK
    cover = ((c_start[:, None] < s_start[None, :] + SLC_BLOCK) & (c_start[:, None] + CMP_BLOCK > s_start[None, :])).astype(jnp.float32)
    imp = jnp.einsum('blgrn,ns->blgs', p_cmp, cover)
    blk = jnp.arange(n_slc, dtype=jnp.int32)[None, :]
    cur = (pos // SLC_BLOCK)[:, None]
    forced = ((blk == 0) | (blk == cur) | (blk == cur - 1))[None, :, None, :]
    allowed = (s_start[None, :] <= pos[:, None])[None, :, None, :]
    score = jnp.where(allowed, jnp.where(forced, FORCE_SCORE, imp), NEG_INF)
    n_top = min(TOP_N, n_slc)
    top_val, top_idx = lax.top_k(score, n_top)
    top_ok = top_val > 0.5 * NEG_INF
    pad = n_slc * SLC_BLOCK - T
    rows = jnp.pad(slc_rows, ((0, 0), (0, pad), (0, 0), (0, 0), (0, 0)))
    rows = rows.reshape(B, n_slc, SLC_BLOCK, 2, NSA_KV_HEADS, NSA_HD).transpose(3, 0, 4, 1, 2, 5)
    k_blk, v_blk = rows[0], rows[1]
    bi = jnp.arange(B)[:, None, None, None]
    gi = jnp.arange(NSA_KV_HEADS)[None, :, None, None]
    qb = SLC_QBLOCK if L % SLC_QBLOCK == 0 else L
    nb = L // qb

    def attend_selected(args):
        q_b, pos_b, idx_b, ok_b = args
        idx_t = idx_b.transpose(0, 2, 1, 3)
        kg = k_blk[bi, gi, idx_t]
        vg = v_blk[bi, gi, idx_t]
        sc = jnp.einsum('bqgrd,bgqnkd->bgqrnk', q_b, kg) * scale
        kpos = idx_t[..., None] * SLC_BLOCK + jnp.arange(SLC_BLOCK, dtype=jnp.int32)
        m = (kpos <= pos_b[None, None, :, None, None]) & ok_b.transpose(0, 2, 1, 3)[..., None]
        m = m[:, :, :, None].reshape(B, NSA_KV_HEADS, qb, 1, n_top * SLC_BLOCK)
        p = masked_softmax(sc.reshape(B, NSA_KV_HEADS, qb, NSA_GROUP, n_top * SLC_BLOCK), m)
        p = p.reshape(B, NSA_KV_HEADS, qb, NSA_GROUP, n_top, SLC_BLOCK).astype(vg.dtype)
        return jnp.einsum('bgqrnk,bgqnkd->bqgrd', p, vg)

    def to_blocks(a):
        return a.reshape(B, nb, qb, *a.shape[2:]).swapaxes(0, 1)

    o_slc = lax.map(attend_selected, (to_blocks(qg), pos.reshape(nb, qb), to_blocks(top_idx), to_blocks(top_ok)))
    o_slc = o_slc.swapaxes(0, 1).reshape(B, L, NSA_HEADS, NSA_HD)
    return o_cmp, o_slc


def band_attend(q, qpos, k, v, kpos):
    B, Q = q.shape[:2]
    qg = q.reshape(B, Q, NSA_KV_HEADS, NSA_GROUP, NSA_HD)
    s = jnp.einsum('bqgrd,bkgd->bgrqk', qg, k) * (NSA_HD ** -0.5)
    d = qpos[:, None] - kpos[None, :]
    mask = (d >= 0) & (d <= WINDOW) & (kpos[None, :] >= 0)
    p = masked_softmax(s, mask).astype(v.dtype)
    o = jnp.einsum('bgrqk,bkgd->bqgrd', p, v)
    return o.reshape(B, Q, NSA_HEADS, NSA_HD)


def window_prompt(q, pos, kw, vw):
    B, L = q.shape[:2]
    qb = WIN_QBLOCK if L % WIN_QBLOCK == 0 else L
    nb = L // qb
    idx = jnp.arange(nb, dtype=jnp.int32)[:, None] * qb + jnp.arange(WINDOW + qb, dtype=jnp.int32)[None, :]
    kpad = jnp.pad(kw, ((0, 0), (WINDOW, 0), (0, 0), (0, 0)))
    vpad = jnp.pad(vw, ((0, 0), (WINDOW, 0), (0, 0), (0, 0)))
    kb = kpad[:, idx]
    vb = vpad[:, idx]
    kpos = idx - WINDOW + pos[0]
    qblk = q.reshape(B, nb, qb, NSA_HEADS, NSA_HD)
    o = jax.vmap(band_attend, in_axes=(1, 0, 1, 1, 0), out_axes=1)(qblk, pos.reshape(nb, qb), kb, vb, kpos)
    return o.reshape(B, L, NSA_HEADS, NSA_HD)


def window_sample(q, pos, kw, vw, buf):
    wb = buf.shape[1]
    keys = jnp.concatenate([buf[:, :, 0], kw.astype(buf.dtype)], axis=1)
    vals = jnp.concatenate([buf[:, :, 1], vw.astype(buf.dtype)], axis=1)
    kpos = jnp.concatenate([pos[0] - wb + jnp.arange(wb, dtype=jnp.int32), pos])
    o = band_attend(q, pos, keys, vals, kpos)
    new_buf = jnp.stack([keys[:, -wb:], vals[:, -wb:]], axis=2)
    return o, new_buf


def token_mixer(h, pos, past, w_in, lb, hg_norm, cmp_pe, cmp_w1, cmp_w2, w_branch, w_out):
    B, L, _ = h.shape
    splits = np.cumsum(IN_SIZES)[:-1].tolist()
    hq, hf, hi, hg, nq, kc, vc, ks, vs, kw, vw, ngate, mgate = jnp.split(h @ w_in, splits, axis=-1)

    def heads(a, n):
        return a.reshape(B, L, n, -1)

    if past is None:
        s0 = jnp.zeros((B, HG_HEADS, HG_DK, HG_DV), jnp.float32)
    else:
        s0 = past[3].astype(jnp.float32)
    o_hg, s_hg = hgrn2_mixer(heads(hq, HG_HEADS), heads(hf, HG_HEADS), heads(hi, HG_HEADS), heads(hg, HG_HEADS), lb, hg_norm, s0)
    q = rope_partial(heads(nq, NSA_HEADS), pos)
    new_cmp = jnp.stack([heads(kc, NSA_KV_HEADS), heads(vc, NSA_KV_HEADS)], axis=2)
    new_slc = jnp.stack([rope_partial(heads(ks, NSA_KV_HEADS), pos), heads(vs, NSA_KV_HEADS)], axis=2)
    kw = rope_partial(heads(kw, NSA_KV_HEADS), pos)
    vw = heads(vw, NSA_KV_HEADS)
    if past is None:
        cmp_rows, slc_rows = new_cmp, new_slc
        o_win = window_prompt(q, pos, kw, vw)
        new_win = jnp.stack([kw, vw], axis=2)[:, -min(WINDOW, L):]
    else:
        cmp_rows = jnp.concatenate([past[0], new_cmp.astype(past[0].dtype)], axis=1)
        slc_rows = jnp.concatenate([past[1], new_slc.astype(past[1].dtype)], axis=1)
        o_win, new_win = window_sample(q, pos, kw, vw, past[2])
    o_cmp, o_slc = nsa_cmp_slc(q, pos, cmp_rows, slc_rows, cmp_pe, cmp_w1, cmp_w2)
    g = jax.nn.sigmoid(ngate.astype(jnp.float32)).reshape(B, L, 3, NSA_HEADS, 1)
    o_nsa = g[:, :, 0] * o_cmp + g[:, :, 1] * o_slc + g[:, :, 2] * o_win
    o_nsa = o_nsa.reshape(B, L, MIX_W).astype(h.dtype)
    branches = jnp.einsum('blnc,ncd->blnd', jnp.stack([o_hg, o_nsa], axis=2), w_branch)
    mg = jax.nn.sigmoid(mgate.astype(jnp.float32)).reshape(B, L, N_BRANCH, D_MODEL).astype(h.dtype)
    merged = jnp.sum(mg * branches, axis=2)
    return merged @ w_out, (new_cmp, new_slc, new_win, s_hg)


def conv_ffn(h, conv_past, w_gate_up, conv_w, conv_b, w_down):
    B, L, _ = h.shape
    g, u = jnp.split(h @ w_gate_up, 2, axis=-1)
    if conv_past is None:
        prev = jnp.zeros((B, CONV_W - 1, D_FF), g.dtype)
    else:
        prev = conv_past.astype(g.dtype)
    gfull = jnp.concatenate([prev, g], axis=1)
    conv = lax.conv_general_dilated(gfull, conv_w[:, None, :].astype(gfull.dtype), window_strides=(1,), padding='VALID',
                                    dimension_numbers=('NWC', 'WIO', 'NWC'), feature_group_count=D_FF) + conv_b
    out = (jax.nn.gelu(conv) * u) @ w_down
    return out, gfull[:, -(CONV_W - 1):]


def gather_pages(pool, page_table):
    pages = pool[page_table]
    return pages.reshape(pages.shape[0], pages.shape[1] * pages.shape[2], *pages.shape[3:])


def trunk(x, pos, caches, page_table, weights):
    (norm_mix, w_in, hg_lower, hg_norm, cmp_pe, cmp_w1, cmp_w2, w_branch, w_out,
     norm_ffn, w_gate_up, conv_w, conv_b, w_down, norm_final) = weights
    lbs = jax.nn.softmax(hg_lower.astype(jnp.float32), axis=0)
    lbs = jnp.cumsum(lbs, axis=0) - lbs[0:1]
    out_cmp, out_slc, out_win, out_hg, out_conv = [], [], [], [], []
    for l in range(DEPTH):
        if caches is None:
            past, conv_past = None, None
        else:
            cache_cmp_kv, cache_slc_kv, cache_win_kv, state_hgrn, state_conv = caches
            past = (gather_pages(cache_cmp_kv[l], page_table), gather_pages(cache_slc_kv[l], page_table),
                    cache_win_kv[l], state_hgrn[l])
            conv_past = state_conv[l]
        mix, (c_kv, s_kv, w_kv, s_hg) = token_mixer(rmsnorm(x, norm_mix[l]), pos, past, w_in[l],
                                                     lbs[l].reshape(HG_HEADS, HG_DK), hg_norm[l], cmp_pe[l],
                                                     cmp_w1[l], cmp_w2[l], w_branch[l], w_out[l])
        x = x + mix.astype(x.dtype)
        ff, c_st = conv_ffn(rmsnorm(x, norm_ffn[l]), conv_past, w_gate_up[l], conv_w[l], conv_b[l], w_down[l])
        x = x + ff.astype(x.dtype)
        out_cmp.append(c_kv)
        out_slc.append(s_kv)
        out_win.append(w_kv)
        out_hg.append(s_hg)
        out_conv.append(c_st)
    return rmsnorm(x, norm_final), (jnp.stack(out_cmp), jnp.stack(out_slc), jnp.stack(out_win),
                                    jnp.stack(out_hg), jnp.stack(out_conv))


def setup_inputs(seed: int = 0) -> dict:
    key = jax.random.key(seed)
    k = jax.random.split(key, 24)
    n_pages = PAST_LEN // PAGE_SIZE
    n_pool = (DEC_BATCH * n_pages * 5) // 4
    win_buf = min(WINDOW, PAST_LEN)
    d_in = sum(IN_SIZES)

    def nrm(kk, shape, scale):
        return jax.random.normal(kk, shape, jnp.float32) * scale

    perm = jax.random.permutation(k[0], n_pool)
    page_table = perm[:DEC_BATCH * n_pages].reshape(DEC_BATCH, n_pages).astype(jnp.int32)
    return {
        'x_prompt': nrm(k[1], (BATCH, SEQ, D_MODEL), 1.0),
        'x_sample': nrm(k[2], (DEC_BATCH, DEC_SEQ, D_MODEL), 1.0),
        'cache_cmp_kv': nrm(k[3], (DEPTH, n_pool, PAGE_SIZE, 2, NSA_KV_HEADS, NSA_HD), 1.0),
        'cache_slc_kv': nrm(k[4], (DEPTH, n_pool, PAGE_SIZE, 2, NSA_KV_HEADS, NSA_HD), 1.0),
        'cache_win_kv': nrm(k[5], (DEPTH, DEC_BATCH, win_buf, 2, NSA_KV_HEADS, NSA_HD), 1.0),
        'state_hgrn': nrm(k[6], (DEPTH, DEC_BATCH, HG_HEADS, HG_DK, HG_DV), 0.5),
        'state_conv': nrm(k[7], (DEPTH, DEC_BATCH, CONV_W - 1, D_FF), 1.0),
        'page_table': page_table,
        'norm_mix': 1.0 + nrm(k[8], (DEPTH, D_MODEL), 0.02),
        'w_in': nrm(k[9], (DEPTH, D_MODEL, d_in), D_MODEL ** -0.5),
        'hg_lower': nrm(k[10], (DEPTH, HG_HEADS * HG_DK), 1.0),
        'hg_norm': 1.0 + nrm(k[11], (DEPTH, HG_DV), 0.02),
        'cmp_pe': nrm(k[12], (DEPTH, 2, CMP_BLOCK, NSA_HD), 0.1),
        'cmp_w1': nrm(k[13], (DEPTH, 2, CMP_BLOCK, NSA_HD, NSA_HD), (CMP_BLOCK * NSA_HD) ** -0.5),
        'cmp_w2': nrm(k[14], (DEPTH, 2, NSA_HD, NSA_HD), NSA_HD ** -0.5),
        'w_branch': nrm(k[15], (DEPTH, N_BRANCH, MIX_W, D_MODEL), MIX_W ** -0.5),
        'w_out': nrm(k[16], (DEPTH, D_MODEL, D_MODEL), 0.5 * D_MODEL ** -0.5),
        'norm_ffn': 1.0 + nrm(k[17], (DEPTH, D_MODEL), 0.02),
        'w_gate_up': nrm(k[18], (DEPTH, D_MODEL, 2 * D_FF), D_MODEL ** -0.5),
        'conv_w': nrm(k[19], (DEPTH, CONV_W, D_FF), CONV_W ** -0.5),
        'conv_b': nrm(k[20], (DEPTH, D_FF), 0.02),
        'w_down': nrm(k[21], (DEPTH, D_FF, D_MODEL), 0.5 * D_FF ** -0.5),
        'norm_final': 1.0 + nrm(k[22], (D_MODEL,), 0.02),
    }


def reference(x_prompt, x_sample, cache_cmp_kv, cache_slc_kv, cache_win_kv, state_hgrn, state_conv, page_table,
              norm_mix, w_in, hg_lower, hg_norm, cmp_pe, cmp_w1, cmp_w2, w_branch, w_out,
              norm_ffn, w_gate_up, conv_w, conv_b, w_down, norm_final):
    weights = (norm_mix, w_in, hg_lower, hg_norm, cmp_pe, cmp_w1, cmp_w2, w_branch, w_out,
               norm_ffn, w_gate_up, conv_w, conv_b, w_down, norm_final)
    past_len = page_table.shape[1] * PAGE_SIZE
    pos_p = jnp.arange(x_prompt.shape[1], dtype=jnp.int32)
    pos_s = past_len + jnp.arange(x_sample.shape[1], dtype=jnp.int32)
    y_prompt, st_p = trunk(x_prompt, pos_p, None, None, weights)
    y_sample, st_s = trunk(x_sample, pos_s, (cache_cmp_kv, cache_slc_kv, cache_win_kv, state_hgrn, state_conv),
                           page_table, weights)
    return (y_prompt, y_sample, st_p[0], st_s[0], st_p[1], st_s[1], st_p[2], st_s[2], st_p[3], st_s[3], st_p[4], st_s[4])
```

```python
import functools

import jax
import jax.numpy as jnp
from jax import lax
from jax.experimental import pallas as pl
from jax.experimental.pallas import tpu as pltpu

F32 = jnp.float32
BF16 = jnp.bfloat16

D_MODEL = 1024
MIX_W = 512
HG_DK = 128
HG_HEADS = 4
HG_DV = 128
NSA_HD = 64
NSA_HEADS = 8
NSA_KV_HEADS = 2
NSA_GROUP = 4
KV_W = NSA_KV_HEADS * NSA_HD
CMP_BLOCK = 32
CMP_STRIDE = 16
SLC_BLOCK = 64
TOP_N = 16
WINDOW = 512
ROPE_THETA = 500000.0
ROPE_DIM = 16
D_FF = 2816
CONV_W = 3
PAGE_SIZE = 128
NORM_EPS = 1e-6
NEG_INF = -1e30
LB_FLOOR = 1e-30
FORCE_SCORE = 1e9
ATT_SCALE = NSA_HD ** -0.5

LANES = 128
N_CMP_SLOTS = 128
VMEM_LIMIT = 56 * 1024 * 1024

C_HQ, C_HF, C_HI, C_HG = 0, 512, 1024, 1536
C_MG = 2048
C_NQ = 4096
C_KV = 4608
C_NG = 5376
N_IN = 5632


def _cparams(sem):
    return pltpu.CompilerParams(dimension_semantics=sem, vmem_limit_bytes=VMEM_LIMIT)


def _sigmoid(x):
    return 1.0 / (1.0 + jnp.exp(-x))


def _gelu_tanh(x):
    return 0.5 * x * (1.0 + jnp.tanh(0.7978845608028654 * (x + 0.044715 * (x * x * x))))


def _dot(a, b):
    return jnp.dot(a, b, preferred_element_type=F32)


def _dot_nt(a, b):
    return lax.dot_general(a, b, (((1,), (1,)), ((), ())), preferred_element_type=F32)


def _masked_softmax(s, mask):
    s = jnp.where(mask, s, NEG_INF)
    m = jnp.max(s, axis=-1, keepdims=True)
    e = jnp.where(mask, jnp.exp(s - m), 0.0)
    return e / jnp.maximum(jnp.sum(e, axis=-1, keepdims=True), 1e-30)


def _norm_matmul_kernel(x_ref, g_ref, w_ref, o_ref, xn_ref):
    @pl.when(pl.program_id(1) == 0)
    def _():
        x = x_ref[...]
        ms = jnp.mean(x * x, axis=-1, keepdims=True)
        xn_ref[...] = (x * lax.rsqrt(ms + NORM_EPS) * g_ref[...]).astype(BF16)

    o_ref[...] = _dot(xn_ref[...], w_ref[...])


def _norm_matmul(x, g, w, tm, tn):
    m, k = x.shape
    n = w.shape[1]
    return pl.pallas_call(
        _norm_matmul_kernel,
        grid=(m // tm, n // tn),
        in_specs=[pl.BlockSpec((tm, k), lambda i, j: (i, 0)),
                  pl.BlockSpec((1, k), lambda i, j: (0, 0)),
                  pl.BlockSpec((k, tn), lambda i, j: (0, j))],
        out_specs=pl.BlockSpec((tm, tn), lambda i, j: (i, j)),
        out_shape=jax.ShapeDtypeStruct((m, n), F32),
        scratch_shapes=[pltpu.VMEM((tm, k), BF16)],
        compiler_params=_cparams(("parallel", "arbitrary")),
        name="norm_matmul",
    )(x, g, w)


def _rmsnorm_kernel(x_ref, g_ref, o_ref):
    x = x_ref[...]
    ms = jnp.mean(x * x, axis=-1, keepdims=True)
    o_ref[...] = x * lax.rsqrt(ms + NORM_EPS) * g_ref[...]


def _rmsnorm(x, g, tm):
    m, k = x.shape
    return pl.pallas_call(
        _rmsnorm_kernel,
        grid=(m // tm,),
        in_specs=[pl.BlockSpec((tm, k), lambda i: (i, 0)), pl.BlockSpec((1, k), lambda i: (0, 0))],
        out_specs=pl.BlockSpec((tm, k), lambda i: (i, 0)),
        out_shape=jax.ShapeDtypeStruct((m, k), F32),
        compiler_params=_cparams(("parallel",)),
        name="final_norm",
    )(x, g)


def _rope_tables(pos):
    half = ROPE_DIM // 2
    inv = jnp.float32(ROPE_THETA) ** (-jnp.arange(half, dtype=F32) * 2.0 / ROPE_DIM)
    ang = pos.astype(F32)[:, None] * inv[None, :]
    cos, sin = jnp.cos(ang), jnp.sin(ang)
    t = pos.shape[0]
    z8 = jnp.zeros((t, half), F32)
    rest = NSA_HD - ROPE_DIM
    a = jnp.concatenate([cos, cos, jnp.ones((t, rest), F32)], axis=1)
    bm = jnp.concatenate([-sin, z8, jnp.zeros((t, rest), F32)], axis=1)
    bp = jnp.concatenate([z8, sin, jnp.zeros((t, rest), F32)], axis=1)
    return tuple(jnp.concatenate([v, v], axis=1) for v in (a, bm, bp))


def _rope(x, a, bm, bp):
    half = ROPE_DIM // 2
    return x * a + pltpu.roll(x, LANES - half, 1) * bm + pltpu.roll(x, half, 1) * bp


def _prep_kernel(nq_ref, kc_ref, vc_ref, ks_ref, vs_ref, kw_ref, vw_ref, a_ref, bm_ref, bp_ref,
                 q_ref, cmp_ref, slc_ref, win_ref, kvb_ref):
    a, bm, bp = a_ref[...], bm_ref[...], bp_ref[...]
    for c in range(MIX_W // LANES):
        q_ref[:, c * LANES:(c + 1) * LANES] = _rope(nq_ref[:, c * LANES:(c + 1) * LANES], a, bm, bp)
    cmp_ref[:, :KV_W] = kc_ref[...]
    cmp_ref[:, KV_W:] = vc_ref[...]
    ks = _rope(ks_ref[...], a, bm, bp)
    vs = vs_ref[...]
    slc_ref[:, :KV_W] = ks
    slc_ref[:, KV_W:] = vs
    kw = _rope(kw_ref[...], a, bm, bp)
    vw = vw_ref[...]
    win_ref[:, :KV_W] = kw
    win_ref[:, KV_W:] = vw
    kvb_ref[:, 0 * KV_W:1 * KV_W] = ks.astype(BF16)
    kvb_ref[:, 1 * KV_W:2 * KV_W] = vs.astype(BF16)
    kvb_ref[:, 2 * KV_W:3 * KV_W] = kw.astype(BF16)
    kvb_ref[:, 3 * KV_W:4 * KV_W] = vw.astype(BF16)


def _prep(p, tables, tm):
    m = p.shape[0]
    nt = tables[0].shape[0] // tm
    kvb0 = C_KV // KV_W
    col = lambda c: pl.BlockSpec((tm, KV_W), lambda i, c=c: (i, kvb0 + c))
    tab = pl.BlockSpec((tm, LANES), lambda i: (i % nt, 0))
    row = lambda w: pl.BlockSpec((tm, w), lambda i: (i, 0))
    return pl.pallas_call(
        _prep_kernel,
        grid=(m // tm,),
        in_specs=[pl.BlockSpec((tm, MIX_W), lambda i: (i, C_NQ // MIX_W))] + [col(c) for c in range(6)] + [tab] * 3,
        out_specs=[row(MIX_W), row(2 * KV_W), row(2 * KV_W), row(2 * KV_W), row(4 * KV_W)],
        out_shape=[jax.ShapeDtypeStruct((m, MIX_W), F32)] + [jax.ShapeDtypeStruct((m, 2 * KV_W), F32)] * 3
        + [jax.ShapeDtypeStruct((m, 4 * KV_W), BF16)],
        compiler_params=_cparams(("parallel",)),
        name="prep",
    )(p, p, p, p, p, p, p, *tables)


def _gla_kernel(*refs, chunk, sub, n_chunks, has_s0, mm_dtype):
    if has_s0:
        q_ref, f_ref, i_ref, g_ref, lb_ref, ng_ref, s0_ref, o_ref, so_ref, st_ref, b_s, k_s = refs
    else:
        q_ref, f_ref, i_ref, g_ref, lb_ref, ng_ref, o_ref, so_ref, st_ref, b_s, k_s = refs
    t = pl.program_id(2)

    @pl.when(t == 0)
    def _():
        if has_s0:
            st_ref[...] = s0_ref[0, 0].T
        else:
            st_ref[...] = jnp.zeros_like(st_ref)

    lb = lb_ref[0]
    log_lb = jnp.log(jnp.maximum(lb, LB_FLOOR))
    log_1m = jnp.log1p(-lb)
    one_m = 1.0 - lb
    ng = ng_ref[...]
    row = lax.broadcasted_iota(jnp.int32, (chunk, HG_DK), 0)
    row_s = lax.broadcasted_iota(jnp.int32, (sub, HG_DK), 0)
    pad = LANES - chunk

    def do_chunk(c, carry):
        r0 = pl.multiple_of(c * chunk, chunk)
        fr = f_ref[pl.ds(r0, chunk), :]
        qr = q_ref[pl.ds(r0, chunk), :]
        v = i_ref[pl.ds(r0, chunk), :]
        log_sig = jnp.minimum(fr, 0.0) - jnp.log1p(jnp.exp(-jnp.abs(fr)))
        bb = log_1m + log_sig
        log_f = jnp.maximum(log_lb, bb) + jnp.log1p(jnp.exp(-jnp.abs(log_lb - bb)))
        k = one_m / (1.0 + jnp.exp(fr))
        q = qr * _sigmoid(qr) * (HG_DK ** -0.5)
        b = log_f
        sh = 1
        while sh < chunk:
            b = b + jnp.where(row >= sh, pltpu.roll(b, sh, 0), 0.0)
            sh *= 2
        b_s[...] = b
        k_s[...] = k
        st = st_ref[...]
        o = _dot_nt((q * jnp.exp(b)).astype(mm_dtype), st.astype(mm_dtype))
        parts = []
        for blk in range(chunk // sub):
            lo = blk * sub
            b_i = b[lo:lo + sub]
            q_i = q[lo:lo + sub]

            def diag(s, acc, lo=lo, b_i=b_i, q_i=q_i):
                b_row = b_s[pl.ds(lo + s, 1), :]
                k_row = k_s[pl.ds(lo + s, 1), :]
                v_row = i_ref[pl.ds(r0 + lo + s, 1), :]
                e = jnp.exp(jnp.where(row_s >= s, b_i - b_row, NEG_INF))
                a = jnp.sum(q_i * e * k_row, axis=-1, keepdims=True)
                return acc + a * v_row

            acc = lax.fori_loop(0, sub, diag, jnp.zeros((sub, HG_DV), F32))
            if blk > 0:
                ref_b = b_s[lo - 1:lo, :]
                q_d = q_i * jnp.exp(b_i - ref_b)
                k_d = jnp.where(row < lo, k * jnp.exp(jnp.minimum(ref_b - b, 0.0)), 0.0)
                att = _dot_nt(q_d.astype(mm_dtype), k_d.astype(mm_dtype))
                acc = acc + _dot(att.astype(mm_dtype), v.astype(mm_dtype))
            parts.append(acc)
        o = o + (parts[0] if len(parts) == 1 else jnp.concatenate(parts, axis=0))
        b_last = b_s[chunk - 1:chunk, :]
        k_dec = k * jnp.exp(b_last - b)
        if pad > 0:
            zeros = jnp.zeros((pad, HG_DK), F32)
            v_t = jnp.concatenate([v, zeros], axis=0).T
            k_dec = jnp.concatenate([k_dec, zeros], axis=0)
        else:
            v_t = v.T
        st_ref[...] = jnp.exp(b_last) * st + _dot(v_t.astype(mm_dtype), k_dec.astype(mm_dtype))
        gr = g_ref[pl.ds(r0, chunk), :]
        ms = jnp.mean(o * o, axis=-1, keepdims=True)
        o_ref[pl.ds(r0, chunk), :] = o * lax.rsqrt(ms + NORM_EPS) * ng * (gr * _sigmoid(gr))
        return carry

    lax.fori_loop(0, n_chunks, do_chunk, 0)

    @pl.when(t == pl.num_programs(2) - 1)
    def _():
        so_ref[0, 0] = st_ref[...].T


def _gla(p, lb, ng, s0, batch, seq, rows, chunk, sub, mm_dtype):
    m = p.shape[0]
    nt = seq // rows
    has_s0 = s0 is not None
    col = lambda c0: pl.BlockSpec((rows, HG_DK), lambda b, h, t, c0=c0: (b * nt + t, c0 // HG_DK + h))
    in_specs = [col(C_HQ), col(C_HF), col(C_HI), col(C_HG),
                pl.BlockSpec((1, 1, HG_DK), lambda b, h, t: (h, 0, 0)),
                pl.BlockSpec((1, HG_DV), lambda b, h, t: (0, 0))]
    args = [p, p, p, p, lb, ng]
    if has_s0:
        in_specs.append(pl.BlockSpec((1, 1, HG_DK, HG_DV), lambda b, h, t: (b, h, 0, 0)))
        args.append(s0)
    return pl.pallas_call(
        functools.partial(_gla_kernel, chunk=chunk, sub=sub, n_chunks=rows // chunk, has_s0=has_s0, mm_dtype=mm_dtype),
        grid=(batch, HG_HEADS, nt),
        in_specs=in_specs,
        out_specs=[pl.BlockSpec((rows, HG_DV), lambda b, h, t: (b * nt + t, h)),
                   pl.BlockSpec((1, 1, HG_DK, HG_DV), lambda b, h, t: (b, h, 0, 0))],
        out_shape=[jax.ShapeDtypeStruct((m, MIX_W), F32),
                   jax.ShapeDtypeStruct((batch, HG_HEADS, HG_DK, HG_DV), F32)],
        scratch_shapes=[pltpu.VMEM((HG_DV, HG_DK), F32), pltpu.VMEM((chunk, HG_DK), F32),
                        pltpu.VMEM((chunk, HG_DK), F32)],
        compiler_params=_cparams(("parallel", "parallel", "arbitrary")),
        name="gla",
    )(*args)


def _compress_one(rows_ref, kv, pe_ref, w1_ref, w2_ref):
    n = N_CMP_SLOTS
    acc_a = jnp.zeros((n, KV_W), F32)
    acc_b = jnp.zeros((n, KV_W), F32)
    for j in range(CMP_STRIDE):
        x = rows_ref[pl.ds(j, n, stride=CMP_STRIDE), :]
        acc_a = acc_a + _dot((x + pe_ref[kv, j:j + 1, :]).astype(BF16), w1_ref[kv, j])
        acc_b = acc_b + _dot((x + pe_ref[kv, CMP_STRIDE + j:CMP_STRIDE + j + 1, :]).astype(BF16),
                             w1_ref[kv, CMP_STRIDE + j])
    h = _gelu_tanh(acc_a + pltpu.roll(acc_b, n - 1, 0))
    out = _dot(h.astype(BF16), w2_ref[kv])
    slot = lax.broadcasted_iota(jnp.int32, (n, KV_W), 0)
    return jnp.where(slot < n - 1, out, 0.0)


def _compress(krows_ref, vrows_ref, pe_ref, w1_ref, w2_ref, a_ref, bm_ref, bp_ref):
    k_cmp = _rope(_compress_one(krows_ref, 0, pe_ref, w1_ref, w2_ref), a_ref[...], bm_ref[...], bp_ref[...])
    return k_cmp, _compress_one(vrows_ref, 1, pe_ref, w1_ref, w2_ref)


def _group_query(q_ref, head, lane):
    grp, pair, half = head // NSA_GROUP, head // 2, head % 2
    x = q_ref[:, pair * LANES:(pair + 1) * LANES]
    if half != grp:
        x = pltpu.roll(x, NSA_HD, 1)
    return jnp.where((lane >= grp * NSA_HD) & (lane < (grp + 1) * NSA_HD), x, 0.0)


def _block_scores(psum, qpos, lane):
    w = psum
    for k in range(1, CMP_BLOCK // CMP_STRIDE + SLC_BLOCK // CMP_STRIDE - 1):
        w = w + jnp.where(lane >= k, pltpu.roll(psum, k, 1), 0.0)
    blk = lane // 4
    cur = qpos // SLC_BLOCK
    forced = (blk == 0) | (blk == cur) | (blk == cur - 1)
    allowed = (blk * SLC_BLOCK <= qpos) & (lane % 4 == 3)
    return jnp.where(allowed, jnp.where(forced, FORCE_SCORE, w), NEG_INF), allowed


def _rank_count(score, lane, cnt):
    for s in range(LANES // 4):
        c = 4 * s + 3
        col = score[:, c:c + 1]
        cnt = cnt + jnp.where((col > score) | ((col == score) & (lane > c)), 1.0, 0.0)
    return cnt


def _place(res, head, lane):
    grp, half = head // NSA_GROUP, head % 2
    return res if half == grp else pltpu.roll(res, NSA_HD, 1)


def _gate_tile(g, branch, pair, left):
    c = branch * NSA_HEADS + 2 * pair
    return jnp.where(left, g[:, c:c + 1], g[:, c + 1:c + 2])


def _nsa_prompt_kernel(q_ref, ng_ref, kcr_ref, vcr_ref, kvb_ref, pe_ref, w1_ref, w2_ref, a_ref, bm_ref, bp_ref, exp_ref,
                       o_ref, kc_s, vc_s, *, tq, seq):
    i = pl.program_id(1)

    @pl.when(i == 0)
    def _():
        k_cmp, v_cmp = _compress(kcr_ref, vcr_ref, pe_ref, w1_ref, w2_ref, a_ref, bm_ref, bp_ref)
        kc_s[...] = k_cmp.astype(BF16)
        vc_s[...] = v_cmp.astype(BF16)

    q0 = i * tq
    lane = lax.broadcasted_iota(jnp.int32, (tq, LANES), 1)
    left = lane < NSA_HD
    qpos = q0 + lax.broadcasted_iota(jnp.int32, (tq, LANES), 0)
    g = _sigmoid(ng_ref[...])
    qm = [_group_query(q_ref, h, lane).astype(BF16) for h in range(NSA_HEADS)]
    kc, vc = kc_s[...], vc_s[...]
    cmask = (CMP_STRIDE * lane + CMP_BLOCK - 1) <= qpos

    kpos = lax.broadcasted_iota(jnp.int32, (tq, seq), 1)
    qpos_k = q0 + lax.broadcasted_iota(jnp.int32, (tq, seq), 0)
    causal = kpos <= qpos_k
    n_win = WINDOW + tq
    w0 = pl.multiple_of(jnp.maximum(q0 - WINDOW, 0), tq)
    dwin = (q0 - w0) + lax.broadcasted_iota(jnp.int32, (tq, n_win), 0) - lax.broadcasted_iota(jnp.int32, (tq, n_win), 1)
    wmask = (dwin >= 0) & (dwin <= WINDOW)
    k_slc = kvb_ref[:, 0 * KV_W:1 * KV_W]
    v_slc = kvb_ref[:, 1 * KV_W:2 * KV_W]
    k_win = kvb_ref[pl.ds(w0, n_win), 2 * KV_W:3 * KV_W]
    v_win = kvb_ref[pl.ds(w0, n_win), 3 * KV_W:4 * KV_W]

    o_cmp, o_slc, o_win = [None] * NSA_HEADS, [None] * NSA_HEADS, [None] * NSA_HEADS
    for grp in range(NSA_KV_HEADS):
        heads = range(grp * NSA_GROUP, (grp + 1) * NSA_GROUP)
        psum = jnp.zeros((tq, LANES), F32)
        for h in heads:
            p = _masked_softmax(_dot_nt(qm[h], kc) * ATT_SCALE, cmask)
            psum = psum + p
            o_cmp[h] = _place(_dot(p.astype(BF16), vc), h, lane)
        score, allowed = _block_scores(psum, qpos, lane)
        cnt = _rank_count(score, lane, jnp.zeros((tq, LANES), F32))
        sel = jnp.where(allowed & (cnt < TOP_N), 1.0, 0.0).astype(BF16)
        kmask = (_dot(sel, exp_ref[...]) > 0.5) & causal
        for h in heads:
            p = _masked_softmax(_dot_nt(qm[h], k_slc) * ATT_SCALE, kmask)
            o_slc[h] = _place(_dot(p.astype(BF16), v_slc), h, lane)
            p = _masked_softmax(_dot_nt(qm[h], k_win) * ATT_SCALE, wmask)
            o_win[h] = _place(_dot(p.astype(BF16), v_win), h, lane)
    for pair in range(NSA_HEADS // 2):
        ha, hb = 2 * pair, 2 * pair + 1
        o_ref[:, pair * LANES:(pair + 1) * LANES] = (
            _gate_tile(g, 0, pair, left) * jnp.where(left, o_cmp[ha], o_cmp[hb])
            + _gate_tile(g, 1, pair, left) * jnp.where(left, o_slc[ha], o_slc[hb])
            + _gate_tile(g, 2, pair, left) * jnp.where(left, o_win[ha], o_win[hb]))


def _select_expand(n_keys):
    lane = jnp.arange(LANES, dtype=jnp.int32)[:, None]
    key = jnp.arange(n_keys, dtype=jnp.int32)[None, :]
    return ((lane % 4 == 3) & (lane // 4 == key // SLC_BLOCK)).astype(BF16)


def _nsa_prompt(q, p, cmp_rows, kvb, cw, batch, seq, tq):
    m = q.shape[0]
    assert seq == CMP_STRIDE * N_CMP_SLOTS and seq % tq == 0 and tq == LANES
    nq = seq // tq
    pe, w1, w2, tabs = cw
    full = lambda a: pl.BlockSpec(a.shape, lambda b, i, nd=a.ndim: (0,) * nd)
    expand = _select_expand(seq)
    return pl.pallas_call(
        functools.partial(_nsa_prompt_kernel, tq=tq, seq=seq),
        grid=(batch, nq),
        in_specs=[pl.BlockSpec((tq, MIX_W), lambda b, i: (b * nq + i, 0)),
                  pl.BlockSpec((tq, LANES), lambda b, i: (b * nq + i, C_NG // LANES)),
                  pl.BlockSpec((seq, KV_W), lambda b, i: (b, 0)),
                  pl.BlockSpec((seq, KV_W), lambda b, i: (b, 1)),
                  pl.BlockSpec((seq, 4 * KV_W), lambda b, i: (b, 0)),
                  full(pe), full(w1), full(w2), full(tabs[0]), full(tabs[1]), full(tabs[2]), full(expand)],
        out_specs=pl.BlockSpec((tq, MIX_W), lambda b, i: (b * nq + i, 0)),
        out_shape=jax.ShapeDtypeStruct((m, MIX_W), F32),
        scratch_shapes=[pltpu.VMEM((N_CMP_SLOTS, KV_W), BF16), pltpu.VMEM((N_CMP_SLOTS, KV_W), BF16)],
        compiler_params=_cparams(("parallel", "arbitrary")),
        name="nsa_prompt",
    )(q, p, cmp_rows, cmp_rows, kvb, pe, w1, w2, *tabs, expand)


def _nsa_sample_kernel(pt_ref, cpage_ref, spage_ref, wbuf_ref, q_ref, ng_ref, slc_ref, win_ref,
                       pe_ref, w1_ref, w2_ref, a_ref, bm_ref, bp_ref, exp_ref,
                       o_ref, wout_ref, kcr_s, vcr_s, slc_s, *, past, dec):
    del pt_ref
    pg = pl.program_id(1)
    r0 = pl.multiple_of(pg * PAGE_SIZE, PAGE_SIZE)
    kcr_s[pl.ds(r0, PAGE_SIZE), :] = cpage_ref[0, :, :KV_W]
    vcr_s[pl.ds(r0, PAGE_SIZE), :] = cpage_ref[0, :, KV_W:]
    slc_s[pl.ds(r0, PAGE_SIZE), :] = spage_ref[0]

    @pl.when(pg == pl.num_programs(1) - 1)
    def _():
        n_rows = NSA_HEADS * dec
        tail = LANES - dec
        slc_s[past:past + LANES, :] = jnp.concatenate([slc_ref[...], jnp.zeros((tail, 2 * KV_W), F32)], axis=0)
        k_cmp, v_cmp = _compress(kcr_s, vcr_s, pe_ref, w1_ref, w2_ref, a_ref, bm_ref, bp_ref)

        lane8 = lax.broadcasted_iota(jnp.int32, (dec, LANES), 1)
        lane = lax.broadcasted_iota(jnp.int32, (n_rows, LANES), 1)
        rowi = lax.broadcasted_iota(jnp.int32, (n_rows, LANES), 0)
        qpos = past + rowi % dec
        qs = jnp.concatenate([_group_query(q_ref, h, lane8) for h in range(NSA_HEADS)], axis=0).astype(BF16)

        cmask = (CMP_STRIDE * lane + CMP_BLOCK - 1) <= qpos
        p = _masked_softmax(_dot_nt(qs, k_cmp.astype(BF16)) * ATT_SCALE, cmask)
        o_cmp = _dot(p.astype(BF16), v_cmp.astype(BF16))

        sels = []
        qpos8 = past + lax.broadcasted_iota(jnp.int32, (dec, LANES), 0)
        for grp in range(NSA_KV_HEADS):
            base = grp * NSA_GROUP * dec
            psum = p[base:base + dec]
            for r in range(1, NSA_GROUP):
                psum = psum + p[base + r * dec:base + (r + 1) * dec]
            score, allowed = _block_scores(psum, qpos8, lane8)
            cnt = _rank_count(score, lane8, jnp.where(score < FORCE_SCORE, 1.0, 0.0))
            sel = jnp.where(allowed & (cnt < TOP_N), 1.0, 0.0)
            sels += [sel] * NSA_GROUP
        sel = jnp.concatenate(sels, axis=0).astype(BF16)
        n_keys = past + LANES
        kpos = lax.broadcasted_iota(jnp.int32, (n_rows, n_keys), 1)
        qpos_k = past + lax.broadcasted_iota(jnp.int32, (n_rows, n_keys), 0) % dec
        kmask = ((_dot(sel, exp_ref[...]) > 0.5) | (kpos >= past)) & (kpos <= qpos_k)
        k_slc = slc_s[:, :KV_W].astype(BF16)
        v_slc = slc_s[:, KV_W:].astype(BF16)
        p = _masked_softmax(_dot_nt(qs, k_slc) * ATT_SCALE, kmask)
        o_slc = _dot(p.astype(BF16), v_slc)

        wb = wbuf_ref.shape[1]
        buf = wbuf_ref[0]
        new = win_ref[...]
        zpad = jnp.zeros((tail, KV_W), F32)
        k_win = jnp.concatenate([buf[:, :KV_W], new[:, :KV_W], zpad], axis=0).astype(BF16)
        v_win = jnp.concatenate([buf[:, KV_W:], new[:, KV_W:], zpad], axis=0).astype(BF16)
        n_w = wb + LANES
        kidx = lax.broadcasted_iota(jnp.int32, (n_rows, n_w), 1)
        kpos_w = past - wb + kidx
        d = past + lax.broadcasted_iota(jnp.int32, (n_rows, n_w), 0) % dec - kpos_w
        wmask = (d >= 0) & (d <= WINDOW) & (kpos_w >= 0)
        p = _masked_softmax(_dot_nt(qs, k_win) * ATT_SCALE, wmask)
        o_win = _dot(p.astype(BF16), v_win)
        wout_ref[0, :wb - dec, :] = buf[dec:, :]
        wout_ref[0, wb - dec:, :] = new

        g = _sigmoid(ng_ref[...])
        left = lane8 < NSA_HD
        for pair in range(NSA_HEADS // 2):
            tile = jnp.zeros((dec, LANES), F32)
            for br, o_b in enumerate((o_cmp, o_slc, o_win)):
                pa = _place(o_b[(2 * pair) * dec:(2 * pair + 1) * dec], 2 * pair, lane8)
                pb = _place(o_b[(2 * pair + 1) * dec:(2 * pair + 2) * dec], 2 * pair + 1, lane8)
                tile = tile + _gate_tile(g, br, pair, left) * jnp.where(left, pa, pb)
            o_ref[:, pair * LANES:(pair + 1) * LANES] = tile


def _nsa_sample(q, p, slc_new, win_new, pool_cmp, pool_slc, win_buf, page_table, layer, cw, batch, dec):
    m = q.shape[0]
    n_pages = page_table.shape[1]
    past = n_pages * PAGE_SIZE
    n_pool = pool_cmp.shape[0] // 4
    wb = win_buf.shape[1]
    assert past == CMP_STRIDE * N_CMP_SLOTS and past % SLC_BLOCK == 0 and dec % 8 == 0 and dec <= SLC_BLOCK
    assert (past + dec - CMP_BLOCK) // CMP_STRIDE + 1 == N_CMP_SLOTS - 1 and wb % 8 == 0
    pe, w1, w2, tabs = cw
    full = lambda a: pl.BlockSpec(a.shape, lambda b, g, pt, nd=a.ndim: (0,) * nd)
    expand = _select_expand(past + LANES)
    page = lambda: pl.BlockSpec((1, PAGE_SIZE, 2 * KV_W),
                                lambda b, g, pt: (layer * n_pool + pt[b * n_pages + g], 0, 0))
    rows = lambda w, c=0: pl.BlockSpec((dec, w), lambda b, g, pt, c=c: (b, c))
    grid_spec = pltpu.PrefetchScalarGridSpec(
        num_scalar_prefetch=1,
        grid=(batch, n_pages),
        in_specs=[page(), page(),
                  pl.BlockSpec((1, wb, 2 * KV_W), lambda b, g, pt: (layer * batch + b, 0, 0)),
                  rows(MIX_W), rows(LANES, C_NG // LANES), rows(2 * KV_W), rows(2 * KV_W),
                  full(pe), full(w1), full(w2), full(tabs[0]), full(tabs[1]), full(tabs[2]), full(expand)],
        out_specs=[rows(MIX_W), pl.BlockSpec((1, wb, 2 * KV_W), lambda b, g, pt: (b, 0, 0))],
        scratch_shapes=[pltpu.VMEM((past, KV_W), F32), pltpu.VMEM((past, KV_W), F32),
                        pltpu.VMEM((past + LANES, 2 * KV_W), F32)],
    )
    return pl.pallas_call(
        functools.partial(_nsa_sample_kernel, past=past, dec=dec),
        grid_spec=grid_spec,
        out_shape=[jax.ShapeDtypeStruct((m, MIX_W), F32), jax.ShapeDtypeStruct((batch, wb, 2 * KV_W), F32)],
        compiler_params=_cparams(("parallel", "arbitrary")),
        name="nsa_sample",
    )(page_table.reshape(-1), pool_cmp, pool_slc, win_buf, q, p, slc_new, win_new, pe, w1, w2, *tabs, expand)


def _merge_kernel(x_ref, hg_ref, nsa_ref, ga_ref, gb_ref, wa_ref, wb_ref, wo_ref, o_ref):
    a = _dot(hg_ref[...].astype(BF16), wa_ref[...])
    b = _dot(nsa_ref[...].astype(BF16), wb_ref[...])
    merged = _sigmoid(ga_ref[...]) * a + _sigmoid(gb_ref[...]) * b
    o_ref[...] = x_ref[...] + _dot(merged.astype(BF16), wo_ref[...])


def _merge(x, o_hg, o_nsa, p, w_a, w_b, w_o, tm):
    m = x.shape[0]
    full = lambda a: pl.BlockSpec(a.shape, lambda i: (0, 0))
    return pl.pallas_call(
        _merge_kernel,
        grid=(m // tm,),
        in_specs=[pl.BlockSpec((tm, D_MODEL), lambda i: (i, 0)),
                  pl.BlockSpec((tm, MIX_W), lambda i: (i, 0)),
                  pl.BlockSpec((tm, MIX_W), lambda i: (i, 0)),
                  pl.BlockSpec((tm, D_MODEL), lambda i: (i, C_MG // D_MODEL)),
                  pl.BlockSpec((tm, D_MODEL), lambda i: (i, C_MG // D_MODEL + 1)),
                  full(w_a), full(w_b), full(w_o)],
        out_specs=pl.BlockSpec((tm, D_MODEL), lambda i: (i, 0)),
        out_shape=jax.ShapeDtypeStruct((m, D_MODEL), F32),
        compiler_params=_cparams(("parallel",)),
        name="merge",
    )(x, o_hg, o_nsa, p, p, w_a, w_b, w_o)


def _ffn_kernel(*refs, tm, seq, has_state):
    if has_state:
        x_ref, gn_ref, wg_ref, wu_ref, cw_ref, cb_ref, wd_ref, p1_ref, p2_ref, o_ref, gt_ref, xn_ref, acc_ref = refs
    else:
        x_ref, gn_ref, wg_ref, wu_ref, cw_ref, cb_ref, wd_ref, o_ref, gt_ref, xn_ref, acc_ref, carry_ref = refs
    i, f = pl.program_id(0), pl.program_id(1)

    @pl.when(f == 0)
    def _():
        x = x_ref[...]
        ms = jnp.mean(x * x, axis=-1, keepdims=True)
        xn_ref[...] = (x * lax.rsqrt(ms + NORM_EPS) * gn_ref[...]).astype(BF16)
        acc_ref[...] = jnp.zeros_like(acc_ref)

    xn = xn_ref[...]
    g = _dot(xn, wg_ref[...])
    u = _dot(xn, wu_ref[...])
    row = lax.broadcasted_iota(jnp.int32, g.shape, 0)
    g1 = pltpu.roll(g, 1, 0)
    g2 = pltpu.roll(g, 2, 0)
    if has_state:
        tpos = row % seq
        g1 = jnp.where(tpos == 0, p1_ref[...], g1)
        g2 = jnp.where(tpos < 2, p2_ref[...], g2)
        gt_ref[...] = g
    else:
        first = (i % (seq // tm)) == 0
        prev = jnp.where(first, 0.0, carry_ref[f])
        g1 = jnp.where(row == 0, prev[7:8, :], g1)
        g2 = jnp.where(row == 0, prev[6:7, :], jnp.where(row == 1, prev[7:8, :], g2))
        carry_ref[f] = g[tm - 8:, :]
        gt_ref[0] = g[tm - 8:, :]
    conv = cw_ref[0:1, :] * g2 + cw_ref[1:2, :] * g1 + cw_ref[2:3, :] * g + cb_ref[...]
    acc_ref[...] += _dot((_gelu_tanh(conv) * u).astype(BF16), wd_ref[...])

    @pl.when(f == pl.num_programs(1) - 1)
    def _():
        o_ref[...] = x_ref[...] + acc_ref[...]


def _ffn(x, gn, w_gu, conv_w, conv_b, w_d, prev, seq, tm, tf):
    m = x.shape[0]
    nf = D_FF // tf
    has_state = prev is not None
    in_specs = [pl.BlockSpec((tm, D_MODEL), lambda i, f: (i, 0)),
                pl.BlockSpec((1, D_MODEL), lambda i, f: (0, 0)),
                pl.BlockSpec((D_MODEL, tf), lambda i, f: (0, f)),
                pl.BlockSpec((D_MODEL, tf), lambda i, f: (0, nf + f)),
                pl.BlockSpec((CONV_W, tf), lambda i, f: (0, f)),
                pl.BlockSpec((1, tf), lambda i, f: (0, f)),
                pl.BlockSpec((tf, D_MODEL), lambda i, f: (f, 0))]
    args = [x, gn, w_gu, w_gu, conv_w, conv_b, w_d]
    scratch = [pltpu.VMEM((tm, D_MODEL), BF16), pltpu.VMEM((tm, D_MODEL), F32)]
    if has_state:
        assert tm % seq == 0
        in_specs += [pl.BlockSpec((tm, tf), lambda i, f: (i, f))] * 2
        args += list(prev)
        gt_spec = pl.BlockSpec((tm, tf), lambda i, f: (i, f))
        gt_shape = jax.ShapeDtypeStruct((m, D_FF), F32)
    else:
        assert seq % tm == 0
        gt_spec = pl.BlockSpec((1, 8, tf), lambda i, f: (i, 0, f))
        gt_shape = jax.ShapeDtypeStruct((m // tm, 8, D_FF), F32)
        scratch.append(pltpu.VMEM((nf, 8, tf), F32))
    return pl.pallas_call(
        functools.partial(_ffn_kernel, tm=tm, seq=seq, has_state=has_state),
        grid=(m // tm, nf),
        in_specs=in_specs,
        out_specs=[pl.BlockSpec((tm, D_MODEL), lambda i, f: (i, 0)), gt_spec],
        out_shape=[jax.ShapeDtypeStruct((m, D_MODEL), F32), gt_shape],
        scratch_shapes=scratch,
        compiler_params=_cparams(("arbitrary", "arbitrary")),
        name="ffn",
    )(*args)


def _prepare_weights(w_in, hg_lower, cmp_pe, cmp_w1, cmp_w2, w_branch, w_out, w_gate_up, w_down):
    depth = w_in.shape[0]
    c_rest = 4 * MIX_W
    c_ng = c_rest + MIX_W + 6 * KV_W
    n_gate = 3 * NSA_HEADS
    w_in_p = jnp.concatenate([
        w_in[:, :, :c_rest], w_in[:, :, c_ng + n_gate:], w_in[:, :, c_rest:c_ng], w_in[:, :, c_ng:c_ng + n_gate],
        jnp.zeros((depth, D_MODEL, N_IN - w_in.shape[2]), w_in.dtype)], axis=2).astype(BF16)
    lbs = jax.nn.softmax(hg_lower.astype(F32), axis=0)
    lbs = (jnp.cumsum(lbs, axis=0) - lbs[0:1]).reshape(depth, HG_HEADS, 1, HG_DK)
    w1 = jnp.zeros((depth, 2, CMP_BLOCK, KV_W, KV_W), F32)
    w2 = jnp.zeros((depth, 2, KV_W, KV_W), F32)
    for c in range(NSA_KV_HEADS):
        sl = slice(c * NSA_HD, (c + 1) * NSA_HD)
        w1 = w1.at[:, :, :, sl, sl].set(cmp_w1)
        w2 = w2.at[:, :, sl, sl].set(cmp_w2)
    pe = jnp.concatenate([cmp_pe] * NSA_KV_HEADS, axis=-1)
    return (w_in_p, lbs, pe, w1.astype(BF16), w2.astype(BF16), w_branch.astype(BF16), w_out.astype(BF16),
            w_gate_up.astype(BF16), w_down.astype(BF16))


def _trunk(x, pos, caches, page_table, weights, batch, seq, cfg):
    (norm_mix, w_in_p, lbs, hg_norm, pe, w1, w2, w_br, w_out, norm_ffn, w_gu, conv_w, conv_b, w_down, norm_final) = weights
    depth = w_in_p.shape[0]
    tm = cfg["tm"]
    tables = _rope_tables(pos)
    if seq < tm:
        tables = tuple(jnp.tile(t, (tm // seq, 1)) for t in tables)
    cmp_tabs = _rope_tables(jnp.arange(N_CMP_SLOTS, dtype=jnp.int32) * CMP_STRIDE + CMP_BLOCK - 1)
    outs = [[] for _ in range(5)]
    for l in range(depth):
        p = _norm_matmul(x, norm_mix[l][None], w_in_p[l], tm, cfg["tn"])
        q_rot, cmp_rows, slc_rows, win_rows, kvb = _prep(p, tables, tm)
        s0 = None if caches is None else caches[3][l]
        o_hg, s_hg = _gla(p, lbs[l], hg_norm[l][None], s0, batch, seq, cfg["gla_rows"], cfg["chunk"], cfg["sub"],
                          cfg["gla_mm"])
        cw = (pe[l], w1[l], w2[l], cmp_tabs)
        if caches is None:
            o_nsa = _nsa_prompt(q_rot, p, cmp_rows, kvb, cw, batch, seq, LANES)
            new_win = win_rows.reshape(batch, seq, 2 * KV_W)[:, -min(WINDOW, seq):]
        else:
            o_nsa, new_win = _nsa_sample(q_rot, p, slc_rows, win_rows, caches[0], caches[1], caches[2], page_table, l,
                                         cw, batch, seq)
        x = _merge(x, o_hg, o_nsa, p, w_br[l, 0], w_br[l, 1], w_out[l], tm)
        if caches is None:
            x, g_tail = _ffn(x, norm_ffn[l][None], w_gu[l], conv_w[l], conv_b[l][None], w_down[l], None, seq,
                             cfg["ffn_tm"], cfg["tf"])
            per = seq // cfg["ffn_tm"]
            c_st = g_tail[per - 1::per, 8 - (CONV_W - 1):, :]
        else:
            st = caches[4][l]
            prev2 = jnp.pad(st, ((0, 0), (0, seq - (CONV_W - 1)), (0, 0))).reshape(batch * seq, D_FF)
            prev1 = jnp.pad(st[:, 1:], ((0, 0), (0, seq - 1), (0, 0))).reshape(batch * seq, D_FF)
            x, g_all = _ffn(x, norm_ffn[l][None], w_gu[l], conv_w[l], conv_b[l][None], w_down[l], (prev1, prev2), seq,
                            cfg["ffn_tm"], cfg["tf"])
            c_st = g_all.reshape(batch, seq, D_FF)[:, seq - (CONV_W - 1):]
        kv6 = lambda a: a.reshape(batch, -1, 2, NSA_KV_HEADS, NSA_HD)
        outs[0].append(kv6(cmp_rows))
        outs[1].append(kv6(slc_rows))
        outs[2].append(kv6(new_win))
        outs[3].append(s_hg)
        outs[4].append(c_st)
    y = _rmsnorm(x, norm_final[None], tm).reshape(batch, seq, D_MODEL)
    return (y,) + tuple(jnp.stack(o) for o in outs)


def kernel(x_prompt, x_sample, cache_cmp_kv, cache_slc_kv, cache_win_kv, state_hgrn, state_conv, page_table, norm_mix, w_in, hg_lower, hg_norm, cmp_pe, cmp_w1, cmp_w2, w_branch, w_out, norm_ffn, w_gate_up, conv_w, conv_b, w_down, norm_final):
    (w_in_p, lbs, pe, w1, w2, w_br, w_o, w_gu, w_d) = _prepare_weights(
        w_in, hg_lower, cmp_pe, cmp_w1, cmp_w2, w_branch, w_out, w_gate_up, w_down)
    weights = (norm_mix, w_in_p, lbs, hg_norm, pe, w1, w2, w_br, w_o, norm_ffn, w_gu, conv_w, conv_b, w_d, norm_final)
    b_p, l_p, _ = x_prompt.shape
    b_s, l_s, _ = x_sample.shape
    depth, n_pool = cache_cmp_kv.shape[:2]
    past = page_table.shape[1] * PAGE_SIZE
    pos_p = jnp.arange(l_p, dtype=jnp.int32)
    pos_s = past + jnp.arange(l_s, dtype=jnp.int32)
    cfg_p = dict(tm=1024, tn=512, gla_rows=256, chunk=64, sub=16, gla_mm=BF16, ffn_tm=512, tf=1408)
    cfg_s = dict(tm=256, tn=512, gla_rows=l_s, chunk=l_s, sub=l_s, gla_mm=F32, ffn_tm=256, tf=1408)
    res_p = _trunk(x_prompt.reshape(b_p * l_p, D_MODEL), pos_p, None, None, weights, b_p, l_p, cfg_p)
    caches = (cache_cmp_kv.reshape(depth * n_pool, PAGE_SIZE, 2 * KV_W),
              cache_slc_kv.reshape(depth * n_pool, PAGE_SIZE, 2 * KV_W),
              cache_win_kv.reshape(depth * b_s, -1, 2 * KV_W), state_hgrn, state_conv)
    res_s = _trunk(x_sample.reshape(b_s * l_s, D_MODEL), pos_s, caches, page_table, weights, b_s, l_s, cfg_s)
    out = []
    for a, b in zip(res_p, res_s):
        out += [a, b]
    out[7] = out[7].reshape(depth, b_s, -1, 2, NSA_KV_HEADS, NSA_HD)
    return tuple(out)
```

```python
import functools

import jax
import jax.numpy as jnp
from jax import lax
from jax.experimental import pallas as pl
from jax.experimental.pallas import tpu as pltpu

F32 = jnp.float32
BF16 = jnp.bfloat16

D_MODEL = 1024
MIX_W = 512
HG_DK = 128
HG_HEADS = 4
HG_DV = 128
NSA_HD = 64
NSA_HEADS = 8
NSA_KV_HEADS = 2
NSA_GROUP = 4
KV_W = NSA_KV_HEADS * NSA_HD
CMP_BLOCK = 32
CMP_STRIDE = 16
SLC_BLOCK = 64
TOP_N = 16
WINDOW = 512
ROPE_THETA = 500000.0
ROPE_DIM = 16
D_FF = 2816
CONV_W = 3
PAGE_SIZE = 128
NORM_EPS = 1e-6
NEG_INF = -1e30
LB_FLOOR = 1e-30
FORCE_SCORE = 1e9
ATT_SCALE = NSA_HD ** -0.5

LANES = 128
N_CMP_SLOTS = 128
SLC_CHUNK = 512
VMEM_LIMIT = 56 * 1024 * 1024

C_HQ, C_HF, C_HI, C_HG = 0, 512, 1024, 1536
C_MG = 2048
C_NQ = 4096
C_KV = 4608
C_NG = 5376
N_IN = 5632


def _cparams(sem):
    return pltpu.CompilerParams(dimension_semantics=sem, vmem_limit_bytes=VMEM_LIMIT)


def _sigmoid(x):
    return 1.0 / (1.0 + jnp.exp(-x))


def _gelu_tanh(x):
    return 0.5 * x * (1.0 + jnp.tanh(0.7978845608028654 * (x + 0.044715 * (x * x * x))))


def _dot(a, b):
    return jnp.dot(a, b, preferred_element_type=F32)


def _dot_nt(a, b):
    return lax.dot_general(a, b, (((1,), (1,)), ((), ())), preferred_element_type=F32)


def _masked_softmax(s, mask):
    s = jnp.where(mask, s, NEG_INF)
    m = jnp.max(s, axis=-1, keepdims=True)
    e = jnp.where(mask, jnp.exp(s - m), 0.0)
    return e * (1.0 / jnp.maximum(jnp.sum(e, axis=-1, keepdims=True), 1e-30))


def _norm_matmul_kernel(x_ref, g_ref, w_ref, o_ref, xn_ref):
    @pl.when(pl.program_id(1) == 0)
    def _():
        x = x_ref[...]
        ms = jnp.mean(x * x, axis=-1, keepdims=True)
        xn_ref[...] = (x * lax.rsqrt(ms + NORM_EPS) * g_ref[...]).astype(BF16)

    o_ref[...] = _dot(xn_ref[...], w_ref[...])


def _norm_matmul(x, g, w, tm, tn):
    m, k = x.shape
    n = w.shape[1]
    return pl.pallas_call(
        _norm_matmul_kernel,
        grid=(m // tm, n // tn),
        in_specs=[pl.BlockSpec((tm, k), lambda i, j: (i, 0)),
                  pl.BlockSpec((1, k), lambda i, j: (0, 0)),
                  pl.BlockSpec((k, tn), lambda i, j: (0, j))],
        out_specs=pl.BlockSpec((tm, tn), lambda i, j: (i, j)),
        out_shape=jax.ShapeDtypeStruct((m, n), F32),
        scratch_shapes=[pltpu.VMEM((tm, k), BF16)],
        compiler_params=_cparams(("parallel", "arbitrary")),
        name="norm_matmul",
    )(x, g, w)


def _rmsnorm_kernel(x_ref, g_ref, o_ref):
    x = x_ref[...]
    ms = jnp.mean(x * x, axis=-1, keepdims=True)
    o_ref[...] = x * lax.rsqrt(ms + NORM_EPS) * g_ref[...]


def _rmsnorm(x, g, tm):
    m, k = x.shape
    return pl.pallas_call(
        _rmsnorm_kernel,
        grid=(m // tm,),
        in_specs=[pl.BlockSpec((tm, k), lambda i: (i, 0)), pl.BlockSpec((1, k), lambda i: (0, 0))],
        out_specs=pl.BlockSpec((tm, k), lambda i: (i, 0)),
        out_shape=jax.ShapeDtypeStruct((m, k), F32),
        compiler_params=_cparams(("parallel",)),
        name="final_norm",
    )(x, g)


def _rope_tables(pos):
    half = ROPE_DIM // 2
    inv = jnp.float32(ROPE_THETA) ** (-jnp.arange(half, dtype=F32) * 2.0 / ROPE_DIM)
    ang = pos.astype(F32)[:, None] * inv[None, :]
    cos, sin = jnp.cos(ang), jnp.sin(ang)
    t = pos.shape[0]
    z8 = jnp.zeros((t, half), F32)
    rest = NSA_HD - ROPE_DIM
    a = jnp.concatenate([cos, cos, jnp.ones((t, rest), F32)], axis=1)
    bm = jnp.concatenate([-sin, z8, jnp.zeros((t, rest), F32)], axis=1)
    bp = jnp.concatenate([z8, sin, jnp.zeros((t, rest), F32)], axis=1)
    return tuple(jnp.concatenate([v, v], axis=1) for v in (a, bm, bp))


def _rope(x, a, bm, bp):
    half = ROPE_DIM // 2
    return x * a + pltpu.roll(x, LANES - half, 1) * bm + pltpu.roll(x, half, 1) * bp


def _prep_kernel(nq_ref, kc_ref, vc_ref, ks_ref, vs_ref, kw_ref, vw_ref, a_ref, bm_ref, bp_ref,
                 q_ref, cmp_ref, slc_ref, win_ref, kvb_ref):
    a, bm, bp = a_ref[...], bm_ref[...], bp_ref[...]
    for c in range(MIX_W // LANES):
        q_ref[:, c * LANES:(c + 1) * LANES] = _rope(nq_ref[:, c * LANES:(c + 1) * LANES], a, bm, bp)
    cmp_ref[:, :KV_W] = kc_ref[...]
    cmp_ref[:, KV_W:] = vc_ref[...]
    ks = _rope(ks_ref[...], a, bm, bp)
    vs = vs_ref[...]
    slc_ref[:, :KV_W] = ks
    slc_ref[:, KV_W:] = vs
    kw = _rope(kw_ref[...], a, bm, bp)
    vw = vw_ref[...]
    win_ref[:, :KV_W] = kw
    win_ref[:, KV_W:] = vw
    kvb_ref[:, 0 * KV_W:1 * KV_W] = ks.astype(BF16)
    kvb_ref[:, 1 * KV_W:2 * KV_W] = vs.astype(BF16)
    kvb_ref[:, 2 * KV_W:3 * KV_W] = kw.astype(BF16)
    kvb_ref[:, 3 * KV_W:4 * KV_W] = vw.astype(BF16)


def _prep(p, tables, tm):
    m = p.shape[0]
    nt = tables[0].shape[0] // tm
    kvb0 = C_KV // KV_W
    col = lambda c: pl.BlockSpec((tm, KV_W), lambda i, c=c: (i, kvb0 + c))
    tab = pl.BlockSpec((tm, LANES), lambda i: (i % nt, 0))
    row = lambda w: pl.BlockSpec((tm, w), lambda i: (i, 0))
    return pl.pallas_call(
        _prep_kernel,
        grid=(m // tm,),
        in_specs=[pl.BlockSpec((tm, MIX_W), lambda i: (i, C_NQ // MIX_W))] + [col(c) for c in range(6)] + [tab] * 3,
        out_specs=[row(MIX_W), row(2 * KV_W), row(2 * KV_W), row(2 * KV_W), row(4 * KV_W)],
        out_shape=[jax.ShapeDtypeStruct((m, MIX_W), F32)] + [jax.ShapeDtypeStruct((m, 2 * KV_W), F32)] * 3
        + [jax.ShapeDtypeStruct((m, 4 * KV_W), BF16)],
        compiler_params=_cparams(("parallel",)),
        name="prep",
    )(p, p, p, p, p, p, p, *tables)


def _gla_kernel(*refs, chunk, sub, n_chunks, has_s0, mm_dtype, heads):
    if has_s0:
        q_ref, f_ref, i_ref, g_ref, lb_ref, ng_ref, s0_ref, o_ref, so_ref, st_ref, b_s, k_s = refs
    else:
        q_ref, f_ref, i_ref, g_ref, lb_ref, ng_ref, o_ref, so_ref, st_ref, b_s, k_s = refs
        s0_ref = None
    for hh in range(heads):
        cols = slice(hh * HG_DK, (hh + 1) * HG_DK)
        _gla_head(q_ref.at[:, cols], f_ref.at[:, cols], i_ref.at[:, cols], g_ref.at[:, cols], lb_ref[hh], ng_ref[...],
                  None if s0_ref is None else s0_ref.at[0, hh], o_ref.at[:, cols], so_ref.at[0, hh], st_ref.at[hh],
                  b_s, k_s, chunk=chunk, sub=sub, n_chunks=n_chunks, mm_dtype=mm_dtype)


def _gla_head(q_ref, f_ref, i_ref, g_ref, lb, ng, s0_ref, o_ref, so_ref, st_ref, b_s, k_s, *,
              chunk, sub, n_chunks, mm_dtype):
    t = pl.program_id(2)

    @pl.when(t == 0)
    def _():
        if s0_ref is not None:
            st_ref[...] = s0_ref[...].T
        else:
            st_ref[...] = jnp.zeros_like(st_ref)

    log_lb = jnp.log(jnp.maximum(lb, LB_FLOOR))
    log_1m = jnp.log1p(-lb)
    one_m = 1.0 - lb
    row = lax.broadcasted_iota(jnp.int32, (chunk, HG_DK), 0)
    row_s = lax.broadcasted_iota(jnp.int32, (sub, HG_DK), 0)
    lane_s = lax.broadcasted_iota(jnp.int32, (sub, LANES), 1)
    pad = LANES - chunk

    def do_chunk(c, carry):
        r0 = pl.multiple_of(c * chunk, chunk)
        fr = f_ref[pl.ds(r0, chunk), :]
        qr = q_ref[pl.ds(r0, chunk), :]
        v = i_ref[pl.ds(r0, chunk), :]
        log_sig = jnp.minimum(fr, 0.0) - jnp.log1p(jnp.exp(-jnp.abs(fr)))
        bb = log_1m + log_sig
        log_f = jnp.maximum(log_lb, bb) + jnp.log1p(jnp.exp(-jnp.abs(log_lb - bb)))
        k = one_m / (1.0 + jnp.exp(fr))
        q = qr * _sigmoid(qr) * (HG_DK ** -0.5)
        b = log_f
        sh = 1
        while sh < chunk:
            b = b + jnp.where(row >= sh, pltpu.roll(b, sh, 0), 0.0)
            sh *= 2
        b_s[...] = b
        k_s[...] = k
        st = st_ref[...]
        o = _dot_nt((q * jnp.exp(b)).astype(mm_dtype), st.astype(mm_dtype))
        pad_rows = lambda a: jnp.concatenate([a, jnp.zeros((pad, HG_DK), F32)], axis=0) if pad > 0 else a
        v_p = pad_rows(v)
        parts = []
        for blk in range(chunk // sub):
            lo = blk * sub
            b_i = b[lo:lo + sub]
            q_i = q[lo:lo + sub]
            if blk > 0:
                ref_b = b_s[lo - 1:lo, :]
                q_d = q_i * jnp.exp(b_i - ref_b)
                k_d = jnp.where(row < lo, k * jnp.exp(jnp.minimum(ref_b - b, 0.0)), 0.0)
                att = _dot_nt(q_d.astype(mm_dtype), pad_rows(k_d).astype(mm_dtype))
            else:
                att = jnp.zeros((sub, LANES), F32)
            zs = []
            for s in range(sub):
                e = jnp.exp(jnp.where(row_s >= s, b_i - b_s[lo + s:lo + s + 1, :], NEG_INF))
                zs.append((q_i * e * k_s[lo + s:lo + s + 1, :]).astype(mm_dtype))
            red = _dot(jnp.concatenate(zs, axis=0), jnp.ones((HG_DK, LANES), mm_dtype))
            for s in range(sub):
                att = jnp.where(lane_s == lo + s, red[s * sub:(s + 1) * sub], att)
            parts.append(att)
        att = parts[0] if len(parts) == 1 else jnp.concatenate(parts, axis=0)
        o = o + _dot(att.astype(mm_dtype), v_p.astype(mm_dtype))
        b_last = b_s[chunk - 1:chunk, :]
        k_dec = pad_rows(k * jnp.exp(b_last - b))
        st_ref[...] = jnp.exp(b_last) * st + _dot(v_p.T.astype(mm_dtype), k_dec.astype(mm_dtype))
        gr = g_ref[pl.ds(r0, chunk), :]
        ms = jnp.mean(o * o, axis=-1, keepdims=True)
        o_ref[pl.ds(r0, chunk), :] = o * lax.rsqrt(ms + NORM_EPS) * ng * (gr * _sigmoid(gr))
        return carry

    lax.fori_loop(0, n_chunks, do_chunk, 0)

    @pl.when(t == pl.num_programs(2) - 1)
    def _():
        so_ref[...] = st_ref[...].T


def _gla(p, lb, ng, s0, batch, seq, rows, chunk, sub, mm_dtype, heads):
    m = p.shape[0]
    nt = seq // rows
    has_s0 = s0 is not None
    w = heads * HG_DK
    col = lambda c0: pl.BlockSpec((rows, w), lambda b, h, t, c0=c0: (b * nt + t, c0 // w + h))
    in_specs = [col(C_HQ), col(C_HF), col(C_HI), col(C_HG),
                pl.BlockSpec((heads, 1, HG_DK), lambda b, h, t: (h, 0, 0)),
                pl.BlockSpec((1, HG_DV), lambda b, h, t: (0, 0))]
    args = [p, p, p, p, lb, ng]
    if has_s0:
        in_specs.append(pl.BlockSpec((1, heads, HG_DK, HG_DV), lambda b, h, t: (b, h, 0, 0)))
        args.append(s0)
    return pl.pallas_call(
        functools.partial(_gla_kernel, chunk=chunk, sub=sub, n_chunks=rows // chunk, has_s0=has_s0, mm_dtype=mm_dtype,
                          heads=heads),
        grid=(batch, HG_HEADS // heads, nt),
        in_specs=in_specs,
        out_specs=[pl.BlockSpec((rows, w), lambda b, h, t: (b * nt + t, h)),
                   pl.BlockSpec((1, heads, HG_DK, HG_DV), lambda b, h, t: (b, h, 0, 0))],
        out_shape=[jax.ShapeDtypeStruct((m, MIX_W), F32),
                   jax.ShapeDtypeStruct((batch, HG_HEADS, HG_DK, HG_DV), F32)],
        scratch_shapes=[pltpu.VMEM((heads, HG_DV, HG_DK), F32), pltpu.VMEM((chunk, HG_DK), F32),
                        pltpu.VMEM((chunk, HG_DK), F32)],
        compiler_params=_cparams(("parallel", "parallel", "arbitrary")),
        name="gla",
    )(*args)


def _compress_one(rows_ref, kv, pe_ref, w1_ref, w2_ref):
    n = N_CMP_SLOTS
    acc_a = jnp.zeros((n, KV_W), F32)
    acc_b = jnp.zeros((n, KV_W), F32)
    for j in range(CMP_STRIDE):
        x = rows_ref[pl.ds(j, n, stride=CMP_STRIDE), :]
        acc_a = acc_a + _dot((x + pe_ref[kv, j:j + 1, :]).astype(BF16), w1_ref[kv, j])
        acc_b = acc_b + _dot((x + pe_ref[kv, CMP_STRIDE + j:CMP_STRIDE + j + 1, :]).astype(BF16),
                             w1_ref[kv, CMP_STRIDE + j])
    h = _gelu_tanh(acc_a + pltpu.roll(acc_b, n - 1, 0))
    out = _dot(h.astype(BF16), w2_ref[kv])
    slot = lax.broadcasted_iota(jnp.int32, (n, KV_W), 0)
    return jnp.where(slot < n - 1, out, 0.0)


def _compress(krows_ref, vrows_ref, pe_ref, w1_ref, w2_ref, a_ref, bm_ref, bp_ref):
    k_cmp = _rope(_compress_one(krows_ref, 0, pe_ref, w1_ref, w2_ref), a_ref[...], bm_ref[...], bp_ref[...])
    return k_cmp, _compress_one(vrows_ref, 1, pe_ref, w1_ref, w2_ref)


def _group_query(q_ref, head, lane):
    grp, pair, half = head // NSA_GROUP, head // 2, head % 2
    x = q_ref[:, pair * LANES:(pair + 1) * LANES]
    if half != grp:
        x = pltpu.roll(x, NSA_HD, 1)
    return jnp.where((lane >= grp * NSA_HD) & (lane < (grp + 1) * NSA_HD), x, 0.0)


def _block_scores(psum, qpos, lane):
    w = psum
    for k in range(1, CMP_BLOCK // CMP_STRIDE + SLC_BLOCK // CMP_STRIDE - 1):
        w = w + jnp.where(lane >= k, pltpu.roll(psum, k, 1), 0.0)
    blk = lane // 4
    cur = qpos // SLC_BLOCK
    forced = (blk == 0) | (blk == cur) | (blk == cur - 1)
    allowed = (blk * SLC_BLOCK <= qpos) & (lane % 4 == 3)
    return jnp.where(allowed, jnp.where(forced, FORCE_SCORE, w), NEG_INF), allowed


def _rank_count(score, lane, cnt):
    for s in range(LANES // 4):
        c = 4 * s + 3
        col = score[:, c:c + 1]
        cnt = cnt + jnp.where((col > score) | ((col == score) & (lane > c)), 1.0, 0.0)
    return cnt


def _place(res, head, lane):
    grp, half = head // NSA_GROUP, head % 2
    return res if half == grp else pltpu.roll(res, NSA_HD, 1)


def _gate_tile(g, branch, pair, left):
    c = branch * NSA_HEADS + 2 * pair
    return jnp.where(left, g[:, c:c + 1], g[:, c + 1:c + 2])


def _nsa_prompt_kernel(q_ref, ng_ref, kcr_ref, vcr_ref, kvb_ref, pe_ref, w1_ref, w2_ref, a_ref, bm_ref, bp_ref, exp_ref,
                       o_ref, kc_s, vc_s, wt_s, sc_s, sel_s, m_s, l_s, acc_s, *, tq, seq):
    i = pl.program_id(1)

    @pl.when(i == 0)
    def _():
        k_cmp, v_cmp = _compress(kcr_ref, vcr_ref, pe_ref, w1_ref, w2_ref, a_ref, bm_ref, bp_ref)
        kc_s[...] = k_cmp.astype(BF16)
        vc_s[...] = v_cmp.astype(BF16)

    q0 = i * tq
    lane = lax.broadcasted_iota(jnp.int32, (tq, LANES), 1)
    left = lane < NSA_HD
    qpos = q0 + lax.broadcasted_iota(jnp.int32, (tq, LANES), 0)
    g = _sigmoid(ng_ref[...])
    qm = [(_group_query(q_ref, h, lane) * ATT_SCALE).astype(BF16) for h in range(NSA_HEADS)]
    kc, vc = kc_s[...], vc_s[...]
    cmask = (CMP_STRIDE * lane + CMP_BLOCK - 1) <= qpos
    n_blk = seq // SLC_BLOCK
    blk_t = lax.broadcasted_iota(jnp.int32, (n_blk, tq), 0)
    qpos_t = q0 + lax.broadcasted_iota(jnp.int32, (n_blk, tq), 1)
    cur_t = qpos_t // SLC_BLOCK
    forced_t = (blk_t == 0) | (blk_t == cur_t) | (blk_t == cur_t - 1)
    allowed_t = blk_t * SLC_BLOCK <= qpos_t

    o_cmp, o_slc, o_win = [None] * NSA_HEADS, [None] * NSA_HEADS, [None] * NSA_HEADS
    for grp in range(NSA_KV_HEADS):
        psum = jnp.zeros((tq, LANES), F32)
        for h in range(grp * NSA_GROUP, (grp + 1) * NSA_GROUP):
            p = _masked_softmax(_dot_nt(qm[h], kc), cmask)
            psum = psum + p
            o_cmp[h] = _place(_dot(p.astype(BF16), vc), h, lane)
        w = psum
        for k in range(1, CMP_BLOCK // CMP_STRIDE + SLC_BLOCK // CMP_STRIDE - 1):
            w = w + jnp.where(lane >= k, pltpu.roll(psum, k, 1), 0.0)
        wt_s[...] = w.T
        imp_t = wt_s[pl.ds(SLC_BLOCK // CMP_STRIDE - 1, n_blk, stride=SLC_BLOCK // CMP_STRIDE), :]
        score = jnp.where(allowed_t, jnp.where(forced_t, FORCE_SCORE, imp_t), NEG_INF)
        sc_s[...] = score
        cnt = jnp.zeros((n_blk, tq), F32)
        for c in range(n_blk):
            other = sc_s[c:c + 1, :]
            cnt = cnt + jnp.where((other > score) | ((other == score) & (blk_t > c)), 1.0, 0.0)
        sel_t = jnp.where(allowed_t & (cnt < TOP_N), 1.0, 0.0)
        sel_s[grp] = jnp.concatenate([sel_t, jnp.zeros((LANES - n_blk, tq), F32)], axis=0).T.astype(BF16)

    for h in range(NSA_HEADS):
        m_s[h] = jnp.full((tq, LANES), NEG_INF, F32)
        l_s[h] = jnp.zeros((tq, LANES), F32)
        acc_s[h] = jnp.zeros((tq, LANES), F32)
    kc_w = SLC_CHUNK
    for ck in range(seq // kc_w):
        @pl.when(ck * kc_w <= q0)
        def _(ck=ck):
            kpos = ck * kc_w + lax.broadcasted_iota(jnp.int32, (tq, kc_w), 1)
            causal = kpos <= q0 + lax.broadcasted_iota(jnp.int32, (tq, kc_w), 0)
            k_c = kvb_ref[ck * kc_w:(ck + 1) * kc_w, 0 * KV_W:1 * KV_W]
            v_c = kvb_ref[ck * kc_w:(ck + 1) * kc_w, 1 * KV_W:2 * KV_W]
            for grp in range(NSA_KV_HEADS):
                picked = _dot(sel_s[grp], exp_ref[:, ck * kc_w:(ck + 1) * kc_w]) > 0.5
                bias = jnp.where(picked & causal, 0.0, NEG_INF)
                for h in range(grp * NSA_GROUP, (grp + 1) * NSA_GROUP):
                    s = _dot_nt(qm[h], k_c) + bias
                    m_prev = m_s[h]
                    m_new = jnp.maximum(m_prev, jnp.max(s, axis=-1, keepdims=True))
                    alpha = jnp.exp(m_prev - m_new)
                    p = jnp.exp(s - jnp.concatenate([m_new] * (kc_w // LANES), axis=1))
                    l_s[h] = alpha * l_s[h] + jnp.sum(p, axis=-1, keepdims=True)
                    acc_s[h] = alpha * acc_s[h] + _dot(p.astype(BF16), v_c)
                    m_s[h] = m_new

    n_win = WINDOW + tq
    w0 = pl.multiple_of(jnp.maximum(q0 - WINDOW, 0), tq)
    dwin = (q0 - w0) + lax.broadcasted_iota(jnp.int32, (tq, n_win), 0) - lax.broadcasted_iota(jnp.int32, (tq, n_win), 1)
    wbias = jnp.where((dwin >= 0) & (dwin <= WINDOW), 0.0, NEG_INF)
    k_win = kvb_ref[pl.ds(w0, n_win), 2 * KV_W:3 * KV_W]
    v_win = kvb_ref[pl.ds(w0, n_win), 3 * KV_W:4 * KV_W]
    for h in range(NSA_HEADS):
        s = _dot_nt(qm[h], k_win) + wbias
        p = jnp.exp(s - jnp.max(s, axis=-1, keepdims=True))
        o = _dot(p.astype(BF16), v_win) * (1.0 / jnp.sum(p, axis=-1, keepdims=True))
        o_win[h] = _place(o, h, lane)
        o_slc[h] = _place(acc_s[h] * (1.0 / l_s[h]), h, lane)
    for pair in range(NSA_HEADS // 2):
        ha, hb = 2 * pair, 2 * pair + 1
        o_ref[:, pair * LANES:(pair + 1) * LANES] = (
            _gate_tile(g, 0, pair, left) * jnp.where(left, o_cmp[ha], o_cmp[hb])
            + _gate_tile(g, 1, pair, left) * jnp.where(left, o_slc[ha], o_slc[hb])
            + _gate_tile(g, 2, pair, left) * jnp.where(left, o_win[ha], o_win[hb]))


def _select_expand(n_keys):
    lane = jnp.arange(LANES, dtype=jnp.int32)[:, None]
    key = jnp.arange(n_keys, dtype=jnp.int32)[None, :]
    return ((lane % 4 == 3) & (lane // 4 == key // SLC_BLOCK)).astype(BF16)


def _nsa_prompt(q, p, cmp_rows, kvb, cw, batch, seq, tq):
    m = q.shape[0]
    assert seq == CMP_STRIDE * N_CMP_SLOTS and seq % tq == 0 and tq == LANES
    nq = seq // tq
    pe, w1, w2, tabs = cw
    full = lambda a: pl.BlockSpec(a.shape, lambda b, i, nd=a.ndim: (0,) * nd)
    key_blk = jnp.arange(seq, dtype=jnp.int32)[None, :] // SLC_BLOCK
    expand = (jnp.arange(LANES, dtype=jnp.int32)[:, None] == key_blk).astype(BF16)
    stat = pltpu.VMEM((NSA_HEADS, tq, LANES), F32)
    return pl.pallas_call(
        functools.partial(_nsa_prompt_kernel, tq=tq, seq=seq),
        grid=(batch, nq),
        in_specs=[pl.BlockSpec((tq, MIX_W), lambda b, i: (b * nq + i, 0)),
                  pl.BlockSpec((tq, LANES), lambda b, i: (b * nq + i, C_NG // LANES)),
                  pl.BlockSpec((seq, KV_W), lambda b, i: (b, 0)),
                  pl.BlockSpec((seq, KV_W), lambda b, i: (b, 1)),
                  pl.BlockSpec((seq, 4 * KV_W), lambda b, i: (b, 0)),
                  full(pe), full(w1), full(w2), full(tabs[0]), full(tabs[1]), full(tabs[2]), full(expand)],
        out_specs=pl.BlockSpec((tq, MIX_W), lambda b, i: (b * nq + i, 0)),
        out_shape=jax.ShapeDtypeStruct((m, MIX_W), F32),
        scratch_shapes=[pltpu.VMEM((N_CMP_SLOTS, KV_W), BF16), pltpu.VMEM((N_CMP_SLOTS, KV_W), BF16),
                        pltpu.VMEM((LANES, tq), F32), pltpu.VMEM((seq // SLC_BLOCK, tq), F32),
                        pltpu.VMEM((NSA_KV_HEADS, tq, LANES), BF16), stat, stat, stat],
        compiler_params=_cparams(("parallel", "arbitrary")),
        name="nsa_prompt",
    )(q, p, cmp_rows, cmp_rows, kvb, pe, w1, w2, *tabs, expand)


def _nsa_sample_kernel(*refs, past, dec, n_pages):
    cpages, spages = refs[1:1 + n_pages], refs[1 + n_pages:1 + 2 * n_pages]
    (wbuf_ref, q_ref, ng_ref, slc_ref, win_ref, pe_ref, w1_ref, w2_ref, a_ref, bm_ref, bp_ref, exp_ref,
     o_ref, wout_ref, kcr_s, vcr_s, kslc_s, vslc_s) = refs[1 + 2 * n_pages:]
    for pg in range(n_pages):
        r0 = pg * PAGE_SIZE
        kcr_s[r0:r0 + PAGE_SIZE, :] = cpages[pg][0, :, :KV_W]
        vcr_s[r0:r0 + PAGE_SIZE, :] = cpages[pg][0, :, KV_W:]
        kslc_s[r0:r0 + PAGE_SIZE, :] = spages[pg][0, :, :KV_W].astype(BF16)
        vslc_s[r0:r0 + PAGE_SIZE, :] = spages[pg][0, :, KV_W:].astype(BF16)

    n_rows = NSA_HEADS * dec
    tail = LANES - dec
    new_slc = jnp.concatenate([slc_ref[...], jnp.zeros((tail, 2 * KV_W), F32)], axis=0).astype(BF16)
    kslc_s[past:past + LANES, :] = new_slc[:, :KV_W]
    vslc_s[past:past + LANES, :] = new_slc[:, KV_W:]
    k_cmp, v_cmp = _compress(kcr_s, vcr_s, pe_ref, w1_ref, w2_ref, a_ref, bm_ref, bp_ref)

    lane8 = lax.broadcasted_iota(jnp.int32, (dec, LANES), 1)
    lane = lax.broadcasted_iota(jnp.int32, (n_rows, LANES), 1)
    rowi = lax.broadcasted_iota(jnp.int32, (n_rows, LANES), 0)
    qpos = past + rowi % dec
    qs = jnp.concatenate([_group_query(q_ref, h, lane8) for h in range(NSA_HEADS)], axis=0).astype(BF16)

    cmask = (CMP_STRIDE * lane + CMP_BLOCK - 1) <= qpos
    p = _masked_softmax(_dot_nt(qs, k_cmp.astype(BF16)) * ATT_SCALE, cmask)
    o_cmp = _dot(p.astype(BF16), v_cmp.astype(BF16))

    sels = []
    qpos8 = past + lax.broadcasted_iota(jnp.int32, (dec, LANES), 0)
    for grp in range(NSA_KV_HEADS):
        base = grp * NSA_GROUP * dec
        psum = p[base:base + dec]
        for r in range(1, NSA_GROUP):
            psum = psum + p[base + r * dec:base + (r + 1) * dec]
        score, allowed = _block_scores(psum, qpos8, lane8)
        cnt = _rank_count(score, lane8, jnp.where(score < FORCE_SCORE, 1.0, 0.0))
        sel = jnp.where(allowed & (cnt < TOP_N), 1.0, 0.0)
        sels += [sel] * NSA_GROUP
    sel = jnp.concatenate(sels, axis=0).astype(BF16)
    n_keys = past + LANES
    kpos = lax.broadcasted_iota(jnp.int32, (n_rows, n_keys), 1)
    qpos_k = past + lax.broadcasted_iota(jnp.int32, (n_rows, n_keys), 0) % dec
    kmask = ((_dot(sel, exp_ref[...]) > 0.5) | (kpos >= past)) & (kpos <= qpos_k)
    p = _masked_softmax(_dot_nt(qs, kslc_s[...]) * ATT_SCALE, kmask)
    o_slc = _dot(p.astype(BF16), vslc_s[...])

    wb = wbuf_ref.shape[1]
    buf = wbuf_ref[0]
    new = win_ref[...]
    zpad = jnp.zeros((tail, KV_W), F32)
    k_win = jnp.concatenate([buf[:, :KV_W], new[:, :KV_W], zpad], axis=0).astype(BF16)
    v_win = jnp.concatenate([buf[:, KV_W:], new[:, KV_W:], zpad], axis=0).astype(BF16)
    n_w = wb + LANES
    kidx = lax.broadcasted_iota(jnp.int32, (n_rows, n_w), 1)
    kpos_w = past - wb + kidx
    d = past + lax.broadcasted_iota(jnp.int32, (n_rows, n_w), 0) % dec - kpos_w
    wmask = (d >= 0) & (d <= WINDOW) & (kpos_w >= 0)
    p = _masked_softmax(_dot_nt(qs, k_win) * ATT_SCALE, wmask)
    o_win = _dot(p.astype(BF16), v_win)
    wout_ref[0, :wb - dec, :] = buf[dec:, :]
    wout_ref[0, wb - dec:, :] = new

    g = _sigmoid(ng_ref[...])
    left = lane8 < NSA_HD
    for pair in range(NSA_HEADS // 2):
        tile = jnp.zeros((dec, LANES), F32)
        for br, o_b in enumerate((o_cmp, o_slc, o_win)):
            pa = _place(o_b[(2 * pair) * dec:(2 * pair + 1) * dec], 2 * pair, lane8)
            pb = _place(o_b[(2 * pair + 1) * dec:(2 * pair + 2) * dec], 2 * pair + 1, lane8)
            tile = tile + _gate_tile(g, br, pair, left) * jnp.where(left, pa, pb)
        o_ref[:, pair * LANES:(pair + 1) * LANES] = tile


def _nsa_sample(q, p, slc_new, win_new, pool_cmp, pool_slc, win_buf, page_table, layer, cw, batch, dec):
    m = q.shape[0]
    n_pages = page_table.shape[1]
    past = n_pages * PAGE_SIZE
    n_pool = pool_cmp.shape[0] // 4
    wb = win_buf.shape[1]
    assert past == CMP_STRIDE * N_CMP_SLOTS and past % SLC_BLOCK == 0 and dec % 8 == 0 and dec <= SLC_BLOCK
    assert (past + dec - CMP_BLOCK) // CMP_STRIDE + 1 == N_CMP_SLOTS - 1 and wb % 8 == 0
    pe, w1, w2, tabs = cw
    full = lambda a: pl.BlockSpec(a.shape, lambda b, pt, nd=a.ndim: (0,) * nd)
    expand = _select_expand(past + LANES)
    page = lambda g: pl.BlockSpec((1, PAGE_SIZE, 2 * KV_W),
                                  lambda b, pt, g=g: (layer * n_pool + pt[b * n_pages + g], 0, 0))
    pages = [page(g) for g in range(n_pages)]
    rows = lambda w, c=0: pl.BlockSpec((dec, w), lambda b, pt, c=c: (b, c))
    grid_spec = pltpu.PrefetchScalarGridSpec(
        num_scalar_prefetch=1,
        grid=(batch,),
        in_specs=pages + pages
        + [pl.BlockSpec((1, wb, 2 * KV_W), lambda b, pt: (layer * batch + b, 0, 0)),
           rows(MIX_W), rows(LANES, C_NG // LANES), rows(2 * KV_W), rows(2 * KV_W),
           full(pe), full(w1), full(w2), full(tabs[0]), full(tabs[1]), full(tabs[2]), full(expand)],
        out_specs=[rows(MIX_W), pl.BlockSpec((1, wb, 2 * KV_W), lambda b, pt: (b, 0, 0))],
        scratch_shapes=[pltpu.VMEM((past, KV_W), F32), pltpu.VMEM((past, KV_W), F32),
                        pltpu.VMEM((past + LANES, KV_W), BF16), pltpu.VMEM((past + LANES, KV_W), BF16)],
    )
    return pl.pallas_call(
        functools.partial(_nsa_sample_kernel, past=past, dec=dec, n_pages=n_pages),
        grid_spec=grid_spec,
        out_shape=[jax.ShapeDtypeStruct((m, MIX_W), F32), jax.ShapeDtypeStruct((batch, wb, 2 * KV_W), F32)],
        compiler_params=_cparams(("parallel",)),
        name="nsa_sample",
    )(page_table.reshape(-1), *([pool_cmp] * n_pages), *([pool_slc] * n_pages), win_buf, q, p, slc_new, win_new,
      pe, w1, w2, *tabs, expand)


def _merge_kernel(x_ref, hg_ref, nsa_ref, ga_ref, gb_ref, wa_ref, wb_ref, wo_ref, o_ref):
    a = _dot(hg_ref[...].astype(BF16), wa_ref[...])
    b = _dot(nsa_ref[...].astype(BF16), wb_ref[...])
    merged = _sigmoid(ga_ref[...]) * a + _sigmoid(gb_ref[...]) * b
    o_ref[...] = x_ref[...] + _dot(merged.astype(BF16), wo_ref[...])


def _merge(x, o_hg, o_nsa, p, w_a, w_b, w_o, tm):
    m = x.shape[0]
    full = lambda a: pl.BlockSpec(a.shape, lambda i: (0, 0))
    return pl.pallas_call(
        _merge_kernel,
        grid=(m // tm,),
        in_specs=[pl.BlockSpec((tm, D_MODEL), lambda i: (i, 0)),
                  pl.BlockSpec((tm, MIX_W), lambda i: (i, 0)),
                  pl.BlockSpec((tm, MIX_W), lambda i: (i, 0)),
                  pl.BlockSpec((tm, D_MODEL), lambda i: (i, C_MG // D_MODEL)),
                  pl.BlockSpec((tm, D_MODEL), lambda i: (i, C_MG // D_MODEL + 1)),
                  full(w_a), full(w_b), full(w_o)],
        out_specs=pl.BlockSpec((tm, D_MODEL), lambda i: (i, 0)),
        out_shape=jax.ShapeDtypeStruct((m, D_MODEL), F32),
        compiler_params=_cparams(("parallel",)),
        name="merge",
    )(x, o_hg, o_nsa, p, p, w_a, w_b, w_o)


def _ffn_kernel(*refs, tm, seq, has_state):
    if has_state:
        x_ref, gn_ref, wg_ref, wu_ref, cw_ref, cb_ref, wd_ref, p1_ref, p2_ref, o_ref, gt_ref, xn_ref, acc_ref = refs
    else:
        x_ref, gn_ref, wg_ref, wu_ref, cw_ref, cb_ref, wd_ref, o_ref, gt_ref, xn_ref, acc_ref, carry_ref = refs
    i, f = pl.program_id(0), pl.program_id(1)

    @pl.when(f == 0)
    def _():
        x = x_ref[...]
        ms = jnp.mean(x * x, axis=-1, keepdims=True)
        xn_ref[...] = (x * lax.rsqrt(ms + NORM_EPS) * gn_ref[...]).astype(BF16)
        acc_ref[...] = jnp.zeros_like(acc_ref)

    xn = xn_ref[...]
    g = _dot(xn, wg_ref[...])
    u = _dot(xn, wu_ref[...])
    row = lax.broadcasted_iota(jnp.int32, g.shape, 0)
    g1 = pltpu.roll(g, 1, 0)
    g2 = pltpu.roll(g, 2, 0)
    if has_state:
        tpos = row % seq
        g1 = jnp.where(tpos == 0, p1_ref[...], g1)
        g2 = jnp.where(tpos < 2, p2_ref[...], g2)
        gt_ref[...] = g
    else:
        first = (i % (seq // tm)) == 0
        prev = jnp.where(first, 0.0, carry_ref[f])
        g1 = jnp.where(row == 0, prev[7:8, :], g1)
        g2 = jnp.where(row == 0, prev[6:7, :], jnp.where(row == 1, prev[7:8, :], g2))
        carry_ref[f] = g[tm - 8:, :]
        gt_ref[0] = g[tm - 8:, :]
    conv = cw_ref[0:1, :] * g2 + cw_ref[1:2, :] * g1 + cw_ref[2:3, :] * g + cb_ref[...]
    acc_ref[...] += _dot((_gelu_tanh(conv) * u).astype(BF16), wd_ref[...])

    @pl.when(f == pl.num_programs(1) - 1)
    def _():
        o_ref[...] = x_ref[...] + acc_ref[...]


def _ffn(x, gn, w_gu, conv_w, conv_b, w_d, prev, seq, tm, tf):
    m = x.shape[0]
    nf = D_FF // tf
    has_state = prev is not None
    in_specs = [pl.BlockSpec((tm, D_MODEL), lambda i, f: (i, 0)),
                pl.BlockSpec((1, D_MODEL), lambda i, f: (0, 0)),
                pl.BlockSpec((D_MODEL, tf), lambda i, f: (0, f)),
                pl.BlockSpec((D_MODEL, tf), lambda i, f: (0, nf + f)),
                pl.BlockSpec((CONV_W, tf), lambda i, f: (0, f)),
                pl.BlockSpec((1, tf), lambda i, f: (0, f)),
                pl.BlockSpec((tf, D_MODEL), lambda i, f: (f, 0))]
    args = [x, gn, w_gu, w_gu, conv_w, conv_b, w_d]
    scratch = [pltpu.VMEM((tm, D_MODEL), BF16), pltpu.VMEM((tm, D_MODEL), F32)]
    if has_state:
        assert tm % seq == 0
        in_specs += [pl.BlockSpec((tm, tf), lambda i, f: (i, f))] * 2
        args += list(prev)
        gt_spec = pl.BlockSpec((tm, tf), lambda i, f: (i, f))
        gt_shape = jax.ShapeDtypeStruct((m, D_FF), F32)
    else:
        assert seq % tm == 0
        gt_spec = pl.BlockSpec((1, 8, tf), lambda i, f: (i, 0, f))
        gt_shape = jax.ShapeDtypeStruct((m // tm, 8, D_FF), F32)
        scratch.append(pltpu.VMEM((nf, 8, tf), F32))
    return pl.pallas_call(
        functools.partial(_ffn_kernel, tm=tm, seq=seq, has_state=has_state),
        grid=(m // tm, nf),
        in_specs=in_specs,
        out_specs=[pl.BlockSpec((tm, D_MODEL), lambda i, f: (i, 0)), gt_spec],
        out_shape=[jax.ShapeDtypeStruct((m, D_MODEL), F32), gt_shape],
        scratch_shapes=scratch,
        compiler_params=_cparams(("arbitrary", "arbitrary")),
        name="ffn",
    )(*args)


def _prepare_weights(w_in, hg_lower, cmp_pe, cmp_w1, cmp_w2, w_branch, w_out, w_gate_up, w_down):
    depth = w_in.shape[0]
    c_rest = 4 * MIX_W
    c_ng = c_rest + MIX_W + 6 * KV_W
    n_gate = 3 * NSA_HEADS
    w_in_p = jnp.concatenate([
        w_in[:, :, :c_rest], w_in[:, :, c_ng + n_gate:], w_in[:, :, c_rest:c_ng], w_in[:, :, c_ng:c_ng + n_gate],
        jnp.zeros((depth, D_MODEL, N_IN - w_in.shape[2]), w_in.dtype)], axis=2).astype(BF16)
    lbs = jax.nn.softmax(hg_lower.astype(F32), axis=0)
    lbs = (jnp.cumsum(lbs, axis=0) - lbs[0:1]).reshape(depth, HG_HEADS, 1, HG_DK)
    w1 = jnp.zeros((depth, 2, CMP_BLOCK, KV_W, KV_W), F32)
    w2 = jnp.zeros((depth, 2, KV_W, KV_W), F32)
    for c in range(NSA_KV_HEADS):
        sl = slice(c * NSA_HD, (c + 1) * NSA_HD)
        w1 = w1.at[:, :, :, sl, sl].set(cmp_w1)
        w2 = w2.at[:, :, sl, sl].set(cmp_w2)
    pe = jnp.concatenate([cmp_pe] * NSA_KV_HEADS, axis=-1)
    return (w_in_p, lbs, pe, w1.astype(BF16), w2.astype(BF16), w_branch.astype(BF16), w_out.astype(BF16),
            w_gate_up.astype(BF16), w_down.astype(BF16))


def _trunk(x, pos, caches, page_table, weights, batch, seq, cfg):
    (norm_mix, w_in_p, lbs, hg_norm, pe, w1, w2, w_br, w_out, norm_ffn, w_gu, conv_w, conv_b, w_down, norm_final) = weights
    depth = w_in_p.shape[0]
    tm = cfg["tm"]
    tables = _rope_tables(pos)
    if seq < tm:
        tables = tuple(jnp.tile(t, (tm // seq, 1)) for t in tables)
    cmp_tabs = _rope_tables(jnp.arange(N_CMP_SLOTS, dtype=jnp.int32) * CMP_STRIDE + CMP_BLOCK - 1)
    outs = [[] for _ in range(5)]
    for l in range(depth):
        p = _norm_matmul(x, norm_mix[l][None], w_in_p[l], tm, cfg["tn"])
        q_rot, cmp_rows, slc_rows, win_rows, kvb = _prep(p, tables, tm)
        s0 = None if caches is None else caches[3][l]
        o_hg, s_hg = _gla(p, lbs[l], hg_norm[l][None], s0, batch, seq, cfg["gla_rows"], cfg["chunk"], cfg["sub"],
                          cfg["gla_mm"], cfg["gla_heads"])
        cw = (pe[l], w1[l], w2[l], cmp_tabs)
        if caches is None:
            o_nsa = _nsa_prompt(q_rot, p, cmp_rows, kvb, cw, batch, seq, LANES)
            new_win = win_rows.reshape(batch, seq, 2 * KV_W)[:, -min(WINDOW, seq):]
        else:
            o_nsa, new_win = _nsa_sample(q_rot, p, slc_rows, win_rows, caches[0], caches[1], caches[2], page_table, l,
                                         cw, batch, seq)
        x = _merge(x, o_hg, o_nsa, p, w_br[l, 0], w_br[l, 1], w_out[l], tm)
        if caches is None:
            x, g_tail = _ffn(x, norm_ffn[l][None], w_gu[l], conv_w[l], conv_b[l][None], w_down[l], None, seq,
                             cfg["ffn_tm"], cfg["tf"])
            per = seq // cfg["ffn_tm"]
            c_st = g_tail[per - 1::per, 8 - (CONV_W - 1):, :]
        else:
            st = caches[4][l]
            prev2 = jnp.pad(st, ((0, 0), (0, seq - (CONV_W - 1)), (0, 0))).reshape(batch * seq, D_FF)
            prev1 = jnp.pad(st[:, 1:], ((0, 0), (0, seq - 1), (0, 0))).reshape(batch * seq, D_FF)
            x, g_all = _ffn(x, norm_ffn[l][None], w_gu[l], conv_w[l], conv_b[l][None], w_down[l], (prev1, prev2), seq,
                            cfg["ffn_tm"], cfg["tf"])
            c_st = g_all.reshape(batch, seq, D_FF)[:, seq - (CONV_W - 1):]
        kv6 = lambda a: a.reshape(batch, -1, 2, NSA_KV_HEADS, NSA_HD)
        outs[0].append(kv6(cmp_rows))
        outs[1].append(kv6(slc_rows))
        outs[2].append(kv6(new_win))
        outs[3].append(s_hg)
        outs[4].append(c_st)
    y = _rmsnorm(x, norm_final[None], tm).reshape(batch, seq, D_MODEL)
    return (y,) + tuple(jnp.stack(o) for o in outs)


def kernel(x_prompt, x_sample, cache_cmp_kv, cache_slc_kv, cache_win_kv, state_hgrn, state_conv, page_table, norm_mix, w_in, hg_lower, hg_norm, cmp_pe, cmp_w1, cmp_w2, w_branch, w_out, norm_ffn, w_gate_up, conv_w, conv_b, w_down, norm_final):
    (w_in_p, lbs, pe, w1, w2, w_br, w_o, w_gu, w_d) = _prepare_weights(
        w_in, hg_lower, cmp_pe, cmp_w1, cmp_w2, w_branch, w_out, w_gate_up, w_down)
    weights = (norm_mix, w_in_p, lbs, hg_norm, pe, w1, w2, w_br, w_o, norm_ffn, w_gu, conv_w, conv_b, w_d, norm_final)
    b_p, l_p, _ = x_prompt.shape
    b_s, l_s, _ = x_sample.shape
    depth, n_pool = cache_cmp_kv.shape[:2]
    past = page_table.shape[1] * PAGE_SIZE
    pos_p = jnp.arange(l_p, dtype=jnp.int32)
    pos_s = past + jnp.arange(l_s, dtype=jnp.int32)
    cfg_p = dict(tm=1024, tn=512, gla_rows=256, chunk=64, sub=16, gla_mm=BF16, gla_heads=1, ffn_tm=512, tf=1408)
    cfg_s = dict(tm=256, tn=512, gla_rows=l_s, chunk=l_s, sub=l_s, gla_mm=F32, gla_heads=HG_HEADS, ffn_tm=256, tf=1408)
    res_p = _trunk(x_prompt.reshape(b_p * l_p, D_MODEL), pos_p, None, None, weights, b_p, l_p, cfg_p)
    caches = (cache_cmp_kv.reshape(depth * n_pool, PAGE_SIZE, 2 * KV_W),
              cache_slc_kv.reshape(depth * n_pool, PAGE_SIZE, 2 * KV_W),
              cache_win_kv.reshape(depth * b_s, -1, 2 * KV_W), state_hgrn, state_conv)
    res_s = _trunk(x_sample.reshape(b_s * l_s, D_MODEL), pos_s, caches, page_table, weights, b_s, l_s, cfg_s)
    out = []
    for a, b in zip(res_p, res_s):
        out += [a, b]
    out[7] = out[7].reshape(depth, b_s, -1, 2, NSA_KV_HEADS, NSA_HD)
    return tuple(out)
```

```python
import functools

import jax
import jax.numpy as jnp
from jax import lax
from jax.experimental import pallas as pl
from jax.experimental.pallas import tpu as pltpu

F32 = jnp.float32
BF16 = jnp.bfloat16

D_MODEL = 1024
MIX_W = 512
HG_DK = 128
HG_HEADS = 4
HG_DV = 128
NSA_HD = 64
NSA_HEADS = 8
NSA_KV_HEADS = 2
NSA_GROUP = 4
KV_W = NSA_KV_HEADS * NSA_HD
CMP_BLOCK = 32
CMP_STRIDE = 16
SLC_BLOCK = 64
TOP_N = 16
WINDOW = 512
ROPE_THETA = 500000.0
ROPE_DIM = 16
D_FF = 2816
CONV_W = 3
PAGE_SIZE = 128
NORM_EPS = 1e-6
NEG_INF = -1e30
LB_FLOOR = 1e-30
FORCE_SCORE = 1e9
ATT_SCALE = NSA_HD ** -0.5
LOG2E = 1.4426950408889634

LANES = 128
N_CMP_SLOTS = 128
SLC_CHUNK = 512
VMEM_LIMIT = 56 * 1024 * 1024

C_HQ, C_HF, C_HI, C_HG = 0, 512, 1024, 1536
C_MG = 2048
C_NQ = 4096
C_KV = 4608
C_NG = 5376
N_IN = 5632


def _cparams(sem):
    return pltpu.CompilerParams(dimension_semantics=sem, vmem_limit_bytes=VMEM_LIMIT)


def _sigmoid(x):
    return 1.0 / (1.0 + jnp.exp(-x))


def _gelu_tanh(x):
    return 0.5 * x * (1.0 + jnp.tanh(0.7978845608028654 * (x + 0.044715 * (x * x * x))))


def _dot(a, b):
    return jnp.dot(a, b, preferred_element_type=F32)


def _dot_nt(a, b):
    return lax.dot_general(a, b, (((1,), (1,)), ((), ())), preferred_element_type=F32)


def _masked_softmax(s, mask):
    s = jnp.where(mask, s, NEG_INF)
    m = jnp.max(s, axis=-1, keepdims=True)
    e = jnp.where(mask, jnp.exp(s - m), 0.0)
    return e * (1.0 / jnp.maximum(jnp.sum(e, axis=-1, keepdims=True), 1e-30))


def _norm_matmul_kernel(x_ref, g_ref, w_ref, o_ref, xn_ref):
    @pl.when(pl.program_id(1) == 0)
    def _():
        x = x_ref[...]
        ms = jnp.mean(x * x, axis=-1, keepdims=True)
        xn_ref[...] = (x * lax.rsqrt(ms + NORM_EPS) * g_ref[...]).astype(BF16)

    o_ref[...] = _dot(xn_ref[...], w_ref[...])


def _norm_matmul(x, g, w, tm, tn):
    m, k = x.shape
    n = w.shape[1]
    return pl.pallas_call(
        _norm_matmul_kernel,
        grid=(m // tm, n // tn),
        in_specs=[pl.BlockSpec((tm, k), lambda i, j: (i, 0)),
                  pl.BlockSpec((1, k), lambda i, j: (0, 0)),
                  pl.BlockSpec((k, tn), lambda i, j: (0, j))],
        out_specs=pl.BlockSpec((tm, tn), lambda i, j: (i, j)),
        out_shape=jax.ShapeDtypeStruct((m, n), F32),
        scratch_shapes=[pltpu.VMEM((tm, k), BF16)],
        compiler_params=_cparams(("parallel", "arbitrary")),
        name="norm_matmul",
    )(x, g, w)


def _rmsnorm_kernel(x_ref, g_ref, o_ref):
    x = x_ref[...]
    ms = jnp.mean(x * x, axis=-1, keepdims=True)
    o_ref[...] = x * lax.rsqrt(ms + NORM_EPS) * g_ref[...]


def _rmsnorm(x, g, tm):
    m, k = x.shape
    return pl.pallas_call(
        _rmsnorm_kernel,
        grid=(m // tm,),
        in_specs=[pl.BlockSpec((tm, k), lambda i: (i, 0)), pl.BlockSpec((1, k), lambda i: (0, 0))],
        out_specs=pl.BlockSpec((tm, k), lambda i: (i, 0)),
        out_shape=jax.ShapeDtypeStruct((m, k), F32),
        compiler_params=_cparams(("parallel",)),
        name="final_norm",
    )(x, g)


def _rope_tables(pos):
    half = ROPE_DIM // 2
    inv = jnp.float32(ROPE_THETA) ** (-jnp.arange(half, dtype=F32) * 2.0 / ROPE_DIM)
    ang = pos.astype(F32)[:, None] * inv[None, :]
    cos, sin = jnp.cos(ang), jnp.sin(ang)
    t = pos.shape[0]
    z8 = jnp.zeros((t, half), F32)
    rest = NSA_HD - ROPE_DIM
    a = jnp.concatenate([cos, cos, jnp.ones((t, rest), F32)], axis=1)
    bm = jnp.concatenate([-sin, z8, jnp.zeros((t, rest), F32)], axis=1)
    bp = jnp.concatenate([z8, sin, jnp.zeros((t, rest), F32)], axis=1)
    return tuple(jnp.concatenate([v, v], axis=1) for v in (a, bm, bp))


def _rope(x, a, bm, bp):
    half = ROPE_DIM // 2
    return x * a + pltpu.roll(x, LANES - half, 1) * bm + pltpu.roll(x, half, 1) * bp


def _prep_kernel(nq_ref, kc_ref, vc_ref, ks_ref, vs_ref, kw_ref, vw_ref, a_ref, bm_ref, bp_ref,
                 q_ref, cmp_ref, slc_ref, win_ref, kvb_ref):
    a, bm, bp = a_ref[...], bm_ref[...], bp_ref[...]
    for c in range(MIX_W // LANES):
        q_ref[:, c * LANES:(c + 1) * LANES] = _rope(nq_ref[:, c * LANES:(c + 1) * LANES], a, bm, bp)
    cmp_ref[:, :KV_W] = kc_ref[...]
    cmp_ref[:, KV_W:] = vc_ref[...]
    ks = _rope(ks_ref[...], a, bm, bp)
    vs = vs_ref[...]
    slc_ref[:, :KV_W] = ks
    slc_ref[:, KV_W:] = vs
    kw = _rope(kw_ref[...], a, bm, bp)
    vw = vw_ref[...]
    win_ref[:, :KV_W] = kw
    win_ref[:, KV_W:] = vw
    kvb_ref[:, 0 * KV_W:1 * KV_W] = ks.astype(BF16)
    kvb_ref[:, 1 * KV_W:2 * KV_W] = vs.astype(BF16)
    kvb_ref[:, 2 * KV_W:3 * KV_W] = kw.astype(BF16)
    kvb_ref[:, 3 * KV_W:4 * KV_W] = vw.astype(BF16)


def _prep(p, tables, tm):
    m = p.shape[0]
    nt = tables[0].shape[0] // tm
    kvb0 = C_KV // KV_W
    col = lambda c: pl.BlockSpec((tm, KV_W), lambda i, c=c: (i, kvb0 + c))
    tab = pl.BlockSpec((tm, LANES), lambda i: (i % nt, 0))
    row = lambda w: pl.BlockSpec((tm, w), lambda i: (i, 0))
    return pl.pallas_call(
        _prep_kernel,
        grid=(m // tm,),
        in_specs=[pl.BlockSpec((tm, MIX_W), lambda i: (i, C_NQ // MIX_W))] + [col(c) for c in range(6)] + [tab] * 3,
        out_specs=[row(MIX_W), row(2 * KV_W), row(2 * KV_W), row(2 * KV_W), row(4 * KV_W)],
        out_shape=[jax.ShapeDtypeStruct((m, MIX_W), F32)] + [jax.ShapeDtypeStruct((m, 2 * KV_W), F32)] * 3
        + [jax.ShapeDtypeStruct((m, 4 * KV_W), BF16)],
        compiler_params=_cparams(("parallel",)),
        name="prep",
    )(p, p, p, p, p, p, p, *tables)


def _gla_kernel(*refs, chunk, sub, n_chunks, has_s0, mm_dtype, heads):
    if has_s0:
        q_ref, f_ref, i_ref, g_ref, lb_ref, ng_ref, s0_ref, o_ref, so_ref, st_ref, b_s, k_s = refs
    else:
        q_ref, f_ref, i_ref, g_ref, lb_ref, ng_ref, o_ref, so_ref, st_ref, b_s, k_s = refs
        s0_ref = None
    for hh in range(heads):
        cols = slice(hh * HG_DK, (hh + 1) * HG_DK)
        _gla_head(q_ref.at[:, cols], f_ref.at[:, cols], i_ref.at[:, cols], g_ref.at[:, cols], lb_ref[hh], ng_ref[...],
                  None if s0_ref is None else s0_ref.at[0, hh], o_ref.at[:, cols], so_ref.at[0, hh], st_ref.at[hh],
                  b_s, k_s, chunk=chunk, sub=sub, n_chunks=n_chunks, mm_dtype=mm_dtype)


def _gla_head(q_ref, f_ref, i_ref, g_ref, lb, ng, s0_ref, o_ref, so_ref, st_ref, b_s, k_s, *,
              chunk, sub, n_chunks, mm_dtype):
    t = pl.program_id(2)

    @pl.when(t == 0)
    def _():
        if s0_ref is not None:
            st_ref[...] = s0_ref[...].T
        else:
            st_ref[...] = jnp.zeros_like(st_ref)

    log_lb = jnp.log(jnp.maximum(lb, LB_FLOOR))
    log_1m = jnp.log1p(-lb)
    one_m = 1.0 - lb
    row = lax.broadcasted_iota(jnp.int32, (chunk, HG_DK), 0)
    row_s = lax.broadcasted_iota(jnp.int32, (sub, HG_DK), 0)
    lane_s = lax.broadcasted_iota(jnp.int32, (sub, LANES), 1)
    pad = LANES - chunk

    def do_chunk(c, carry):
        r0 = pl.multiple_of(c * chunk, chunk)
        fr = f_ref[pl.ds(r0, chunk), :]
        qr = q_ref[pl.ds(r0, chunk), :]
        v = i_ref[pl.ds(r0, chunk), :]
        log_sig = jnp.minimum(fr, 0.0) - jnp.log1p(jnp.exp(-jnp.abs(fr)))
        bb = log_1m + log_sig
        log_f = jnp.maximum(log_lb, bb) + jnp.log1p(jnp.exp(-jnp.abs(log_lb - bb)))
        k = one_m / (1.0 + jnp.exp(fr))
        q = qr * _sigmoid(qr) * (HG_DK ** -0.5)
        b = log_f
        sh = 1
        while sh < chunk:
            b = b + jnp.where(row >= sh, pltpu.roll(b, sh, 0), 0.0)
            sh *= 2
        b_s[...] = b
        k_s[...] = k
        st = st_ref[...]
        o = _dot_nt((q * jnp.exp(b)).astype(mm_dtype), st.astype(mm_dtype))
        pad_rows = lambda a: jnp.concatenate([a, jnp.zeros((pad, HG_DK), F32)], axis=0) if pad > 0 else a
        v_p = pad_rows(v)
        parts = []
        for blk in range(chunk // sub):
            lo = blk * sub
            b_i = b[lo:lo + sub]
            q_i = q[lo:lo + sub]
            if blk > 0:
                ref_b = b_s[lo - 1:lo, :]
                q_d = q_i * jnp.exp(b_i - ref_b)
                k_d = jnp.where(row < lo, k * jnp.exp(jnp.minimum(ref_b - b, 0.0)), 0.0)
                att = _dot_nt(q_d.astype(mm_dtype), pad_rows(k_d).astype(mm_dtype))
            else:
                att = jnp.zeros((sub, LANES), F32)
            zs = []
            for s in range(sub):
                e = jnp.exp(jnp.where(row_s >= s, b_i - b_s[lo + s:lo + s + 1, :], NEG_INF))
                zs.append((q_i * e * k_s[lo + s:lo + s + 1, :]).astype(mm_dtype))
            red = _dot(jnp.concatenate(zs, axis=0), jnp.ones((HG_DK, LANES), mm_dtype))
            for s in range(sub):
                att = jnp.where(lane_s == lo + s, red[s * sub:(s + 1) * sub], att)
            parts.append(att)
        att = parts[0] if len(parts) == 1 else jnp.concatenate(parts, axis=0)
        o = o + _dot(att.astype(mm_dtype), v_p.astype(mm_dtype))
        b_last = b_s[chunk - 1:chunk, :]
        k_dec = pad_rows(k * jnp.exp(b_last - b))
        st_ref[...] = jnp.exp(b_last) * st + _dot(v_p.T.astype(mm_dtype), k_dec.astype(mm_dtype))
        gr = g_ref[pl.ds(r0, chunk), :]
        ms = jnp.mean(o * o, axis=-1, keepdims=True)
        o_ref[pl.ds(r0, chunk), :] = o * lax.rsqrt(ms + NORM_EPS) * ng * (gr * _sigmoid(gr))
        return carry

    lax.fori_loop(0, n_chunks, do_chunk, 0, unroll=True)

    @pl.when(t == pl.num_programs(2) - 1)
    def _():
        so_ref[...] = st_ref[...].T


def _gla(p, lb, ng, s0, batch, seq, rows, chunk, sub, mm_dtype, heads):
    m = p.shape[0]
    nt = seq // rows
    has_s0 = s0 is not None
    w = heads * HG_DK
    col = lambda c0: pl.BlockSpec((rows, w), lambda b, h, t, c0=c0: (b * nt + t, c0 // w + h))
    in_specs = [col(C_HQ), col(C_HF), col(C_HI), col(C_HG),
                pl.BlockSpec((heads, 1, HG_DK), lambda b, h, t: (h, 0, 0)),
                pl.BlockSpec((1, HG_DV), lambda b, h, t: (0, 0))]
    args = [p, p, p, p, lb, ng]
    if has_s0:
        in_specs.append(pl.BlockSpec((1, heads, HG_DK, HG_DV), lambda b, h, t: (b, h, 0, 0)))
        args.append(s0)
    return pl.pallas_call(
        functools.partial(_gla_kernel, chunk=chunk, sub=sub, n_chunks=rows // chunk, has_s0=has_s0, mm_dtype=mm_dtype,
                          heads=heads),
        grid=(batch, HG_HEADS // heads, nt),
        in_specs=in_specs,
        out_specs=[pl.BlockSpec((rows, w), lambda b, h, t: (b * nt + t, h)),
                   pl.BlockSpec((1, heads, HG_DK, HG_DV), lambda b, h, t: (b, h, 0, 0))],
        out_shape=[jax.ShapeDtypeStruct((m, MIX_W), F32),
                   jax.ShapeDtypeStruct((batch, HG_HEADS, HG_DK, HG_DV), F32)],
        scratch_shapes=[pltpu.VMEM((heads, HG_DV, HG_DK), F32), pltpu.VMEM((chunk, HG_DK), F32),
                        pltpu.VMEM((chunk, HG_DK), F32)],
        compiler_params=_cparams(("parallel", "parallel", "arbitrary")),
        name="gla",
    )(*args)


def _compress_one(rows_ref, kv, pe_ref, w1_ref, w2_ref):
    n = N_CMP_SLOTS
    acc_a = jnp.zeros((n, KV_W), F32)
    acc_b = jnp.zeros((n, KV_W), F32)
    for j in range(CMP_STRIDE):
        x = rows_ref[pl.ds(j, n, stride=CMP_STRIDE), :]
        acc_a = acc_a + _dot((x + pe_ref[kv, j:j + 1, :]).astype(BF16), w1_ref[kv, j])
        acc_b = acc_b + _dot((x + pe_ref[kv, CMP_STRIDE + j:CMP_STRIDE + j + 1, :]).astype(BF16),
                             w1_ref[kv, CMP_STRIDE + j])
    h = _gelu_tanh(acc_a + pltpu.roll(acc_b, n - 1, 0))
    out = _dot(h.astype(BF16), w2_ref[kv])
    slot = lax.broadcasted_iota(jnp.int32, (n, KV_W), 0)
    return jnp.where(slot < n - 1, out, 0.0)


def _compress(krows_ref, vrows_ref, pe_ref, w1_ref, w2_ref, a_ref, bm_ref, bp_ref):
    k_cmp = _rope(_compress_one(krows_ref, 0, pe_ref, w1_ref, w2_ref), a_ref[...], bm_ref[...], bp_ref[...])
    return k_cmp, _compress_one(vrows_ref, 1, pe_ref, w1_ref, w2_ref)


def _group_query(q_ref, head, lane):
    grp, pair, half = head // NSA_GROUP, head // 2, head % 2
    x = q_ref[:, pair * LANES:(pair + 1) * LANES]
    if half != grp:
        x = pltpu.roll(x, NSA_HD, 1)
    return jnp.where((lane >= grp * NSA_HD) & (lane < (grp + 1) * NSA_HD), x, 0.0)


def _block_scores(psum, qpos, lane):
    w = psum
    for k in range(1, CMP_BLOCK // CMP_STRIDE + SLC_BLOCK // CMP_STRIDE - 1):
        w = w + jnp.where(lane >= k, pltpu.roll(psum, k, 1), 0.0)
    blk = lane // 4
    cur = qpos // SLC_BLOCK
    forced = (blk == 0) | (blk == cur) | (blk == cur - 1)
    allowed = (blk * SLC_BLOCK <= qpos) & (lane % 4 == 3)
    return jnp.where(allowed, jnp.where(forced, FORCE_SCORE, w), NEG_INF), allowed


def _rank_count(score, lane, cnt):
    for s in range(LANES // 4):
        c = 4 * s + 3
        col = score[:, c:c + 1]
        cnt = cnt + jnp.where((col > score) | ((col == score) & (lane > c)), 1.0, 0.0)
    return cnt


def _place(res, head, lane):
    grp, half = head // NSA_GROUP, head % 2
    return res if half == grp else pltpu.roll(res, NSA_HD, 1)


def _gate_tile(g, branch, pair, left):
    c = branch * NSA_HEADS + 2 * pair
    return jnp.where(left, g[:, c:c + 1], g[:, c + 1:c + 2])


def _nsa_prompt_kernel(q_ref, ng_ref, kcr_ref, vcr_ref, kvb_ref, pe_ref, w1_ref, w2_ref, a_ref, bm_ref, bp_ref, exp_ref,
                       o_ref, kc_s, vc_s, wt_s, sc_s, sel_s, m_s, l_s, acc_s, *, tq, seq):
    i = pl.program_id(1)

    @pl.when(i == 0)
    def _():
        k_cmp, v_cmp = _compress(kcr_ref, vcr_ref, pe_ref, w1_ref, w2_ref, a_ref, bm_ref, bp_ref)
        kc_s[...] = k_cmp.astype(BF16)
        vc_s[...] = v_cmp.astype(BF16)

    q0 = i * tq
    lane = lax.broadcasted_iota(jnp.int32, (tq, LANES), 1)
    left = lane < NSA_HD
    qpos = q0 + lax.broadcasted_iota(jnp.int32, (tq, LANES), 0)
    g = _sigmoid(ng_ref[...])
    qm = [(_group_query(q_ref, h, lane) * (ATT_SCALE * LOG2E)).astype(BF16) for h in range(NSA_HEADS)]
    qg = [jnp.concatenate(qm[grp * NSA_GROUP:(grp + 1) * NSA_GROUP], axis=0) for grp in range(NSA_KV_HEADS)]
    rows_of = lambda a, r: a[r * tq:(r + 1) * tq]
    kc, vc = kc_s[...], vc_s[...]
    cmask = (CMP_STRIDE * lane + CMP_BLOCK - 1) <= qpos
    n_blk = seq // SLC_BLOCK
    blk_t = lax.broadcasted_iota(jnp.int32, (n_blk, tq), 0)
    qpos_t = q0 + lax.broadcasted_iota(jnp.int32, (n_blk, tq), 1)
    cur_t = qpos_t // SLC_BLOCK
    forced_t = (blk_t == 0) | (blk_t == cur_t) | (blk_t == cur_t - 1)
    allowed_t = blk_t * SLC_BLOCK <= qpos_t

    o_cmp, o_slc, o_win = [None] * NSA_HEADS, [None] * NSA_HEADS, [None] * NSA_HEADS
    for grp in range(NSA_KV_HEADS):
        psum = jnp.zeros((tq, LANES), F32)
        s_all = _dot_nt(qg[grp], kc)
        ps = []
        for r in range(NSA_GROUP):
            s = jnp.where(cmask, rows_of(s_all, r), NEG_INF)
            e = jnp.where(cmask, jnp.exp2(s - jnp.max(s, axis=-1, keepdims=True)), 0.0)
            p = e * (1.0 / jnp.maximum(jnp.sum(e, axis=-1, keepdims=True), 1e-30))
            psum = psum + p
            ps.append(p.astype(BF16))
        o_all = _dot(jnp.concatenate(ps, axis=0), vc)
        for r in range(NSA_GROUP):
            o_cmp[grp * NSA_GROUP + r] = _place(rows_of(o_all, r), grp * NSA_GROUP + r, lane)
        w = psum
        for k in range(1, CMP_BLOCK // CMP_STRIDE + SLC_BLOCK // CMP_STRIDE - 1):
            w = w + jnp.where(lane >= k, pltpu.roll(psum, k, 1), 0.0)
        wt_s[...] = w.T
        imp_t = wt_s[pl.ds(SLC_BLOCK // CMP_STRIDE - 1, n_blk, stride=SLC_BLOCK // CMP_STRIDE), :]
        score = jnp.where(allowed_t, jnp.where(forced_t, FORCE_SCORE, imp_t), NEG_INF)
        sc_s[...] = score
        cnt = jnp.zeros((n_blk, tq), F32)
        for c in range(n_blk):
            other = sc_s[c:c + 1, :]
            cnt = cnt + jnp.where((other > score) | ((other == score) & (blk_t > c)), 1.0, 0.0)
        sel_t = jnp.where(allowed_t & (cnt < TOP_N), 1.0, 0.0)
        sel_s[grp] = jnp.concatenate([sel_t, jnp.zeros((LANES - n_blk, tq), F32)], axis=0).T.astype(BF16)

    for h in range(NSA_HEADS):
        m_s[h] = jnp.full((tq, LANES), NEG_INF, F32)
        l_s[h] = jnp.zeros((tq, LANES), F32)
        acc_s[h] = jnp.zeros((tq, LANES), F32)
    kc_w = SLC_CHUNK
    for ck in range(seq // kc_w):
        @pl.when(ck * kc_w <= q0)
        def _(ck=ck):
            kpos = ck * kc_w + lax.broadcasted_iota(jnp.int32, (tq, kc_w), 1)
            causal = kpos <= q0 + lax.broadcasted_iota(jnp.int32, (tq, kc_w), 0)
            k_c = kvb_ref[ck * kc_w:(ck + 1) * kc_w, 0 * KV_W:1 * KV_W]
            v_c = kvb_ref[ck * kc_w:(ck + 1) * kc_w, 1 * KV_W:2 * KV_W]
            for grp in range(NSA_KV_HEADS):
                picked = _dot(sel_s[grp], exp_ref[:, ck * kc_w:(ck + 1) * kc_w]) > 0.5
                bias = jnp.where(picked & causal, 0.0, NEG_INF)
                s_all = _dot_nt(qg[grp], k_c)
                ps, alphas = [], []
                for r in range(NSA_GROUP):
                    h = grp * NSA_GROUP + r
                    s = rows_of(s_all, r) + bias
                    m_prev = m_s[h]
                    m_new = jnp.maximum(m_prev, jnp.max(s, axis=-1, keepdims=True))
                    alpha = jnp.exp2(m_prev - m_new)
                    p = jnp.exp2(s - jnp.concatenate([m_new] * (kc_w // LANES), axis=1))
                    l_s[h] = alpha * l_s[h] + jnp.sum(p, axis=-1, keepdims=True)
                    m_s[h] = m_new
                    ps.append(p.astype(BF16))
                    alphas.append(alpha)
                pv = _dot(jnp.concatenate(ps, axis=0), v_c)
                for r in range(NSA_GROUP):
                    h = grp * NSA_GROUP + r
                    acc_s[h] = alphas[r] * acc_s[h] + rows_of(pv, r)

    n_win = WINDOW + tq
    w0 = pl.multiple_of(jnp.maximum(q0 - WINDOW, 0), tq)
    dwin = (q0 - w0) + lax.broadcasted_iota(jnp.int32, (tq, n_win), 0) - lax.broadcasted_iota(jnp.int32, (tq, n_win), 1)
    wbias = jnp.where((dwin >= 0) & (dwin <= WINDOW), 0.0, NEG_INF)
    k_win = kvb_ref[pl.ds(w0, n_win), 2 * KV_W:3 * KV_W]
    v_win = kvb_ref[pl.ds(w0, n_win), 3 * KV_W:4 * KV_W]
    for grp in range(NSA_KV_HEADS):
        s_all = _dot_nt(qg[grp], k_win)
        ps, inv = [], []
        for r in range(NSA_GROUP):
            s = rows_of(s_all, r) + wbias
            p = jnp.exp2(s - jnp.max(s, axis=-1, keepdims=True))
            inv.append(1.0 / jnp.sum(p, axis=-1, keepdims=True))
            ps.append(p.astype(BF16))
        o_all = _dot(jnp.concatenate(ps, axis=0), v_win)
        for r in range(NSA_GROUP):
            h = grp * NSA_GROUP + r
            o_win[h] = _place(rows_of(o_all, r) * inv[r], h, lane)
            o_slc[h] = _place(acc_s[h] * (1.0 / l_s[h]), h, lane)
    for pair in range(NSA_HEADS // 2):
        ha, hb = 2 * pair, 2 * pair + 1
        o_ref[:, pair * LANES:(pair + 1) * LANES] = (
            _gate_tile(g, 0, pair, left) * jnp.where(left, o_cmp[ha], o_cmp[hb])
            + _gate_tile(g, 1, pair, left) * jnp.where(left, o_slc[ha], o_slc[hb])
            + _gate_tile(g, 2, pair, left) * jnp.where(left, o_win[ha], o_win[hb]))


def _select_expand(n_keys):
    lane = jnp.arange(LANES, dtype=jnp.int32)[:, None]
    key = jnp.arange(n_keys, dtype=jnp.int32)[None, :]
    return ((lane % 4 == 3) & (lane // 4 == key // SLC_BLOCK)).astype(BF16)


def _nsa_prompt(q, p, cmp_rows, kvb, cw, batch, seq, tq):
    m = q.shape[0]
    assert seq == CMP_STRIDE * N_CMP_SLOTS and seq % tq == 0 and tq == LANES
    nq = seq // tq
    pe, w1, w2, tabs = cw
    full = lambda a: pl.BlockSpec(a.shape, lambda b, i, nd=a.ndim: (0,) * nd)
    key_blk = jnp.arange(seq, dtype=jnp.int32)[None, :] // SLC_BLOCK
    expand = (jnp.arange(LANES, dtype=jnp.int32)[:, None] == key_blk).astype(BF16)
    stat = pltpu.VMEM((NSA_HEADS, tq, LANES), F32)
    return pl.pallas_call(
        functools.partial(_nsa_prompt_kernel, tq=tq, seq=seq),
        grid=(batch, nq),
        in_specs=[pl.BlockSpec((tq, MIX_W), lambda b, i: (b * nq + i, 0)),
                  pl.BlockSpec((tq, LANES), lambda b, i: (b * nq + i, C_NG // LANES)),
                  pl.BlockSpec((seq, KV_W), lambda b, i: (b, 0)),
                  pl.BlockSpec((seq, KV_W), lambda b, i: (b, 1)),
                  pl.BlockSpec((seq, 4 * KV_W), lambda b, i: (b, 0)),
                  full(pe), full(w1), full(w2), full(tabs[0]), full(tabs[1]), full(tabs[2]), full(expand)],
        out_specs=pl.BlockSpec((tq, MIX_W), lambda b, i: (b * nq + i, 0)),
        out_shape=jax.ShapeDtypeStruct((m, MIX_W), F32),
        scratch_shapes=[pltpu.VMEM((N_CMP_SLOTS, KV_W), BF16), pltpu.VMEM((N_CMP_SLOTS, KV_W), BF16),
                        pltpu.VMEM((LANES, tq), F32), pltpu.VMEM((seq // SLC_BLOCK, tq), F32),
                        pltpu.VMEM((NSA_KV_HEADS, tq, LANES), BF16), stat, stat, stat],
        compiler_params=_cparams(("parallel", "arbitrary")),
        name="nsa_prompt",
    )(q, p, cmp_rows, cmp_rows, kvb, pe, w1, w2, *tabs, expand)


def _nsa_sample_kernel(*refs, past, dec, n_pages):
    cpages, spages = refs[1:1 + n_pages], refs[1 + n_pages:1 + 2 * n_pages]
    (wbuf_ref, q_ref, ng_ref, slc_ref, win_ref, pe_ref, w1_ref, w2_ref, a_ref, bm_ref, bp_ref, exp_ref,
     o_ref, wout_ref, kcr_s, vcr_s, kslc_s, vslc_s) = refs[1 + 2 * n_pages:]
    for pg in range(n_pages):
        r0 = pg * PAGE_SIZE
        kcr_s[r0:r0 + PAGE_SIZE, :] = cpages[pg][0, :, :KV_W]
        vcr_s[r0:r0 + PAGE_SIZE, :] = cpages[pg][0, :, KV_W:]
        kslc_s[r0:r0 + PAGE_SIZE, :] = spages[pg][0, :, :KV_W].astype(BF16)
        vslc_s[r0:r0 + PAGE_SIZE, :] = spages[pg][0, :, KV_W:].astype(BF16)

    n_rows = NSA_HEADS * dec
    tail = LANES - dec
    new_slc = jnp.concatenate([slc_ref[...], jnp.zeros((tail, 2 * KV_W), F32)], axis=0).astype(BF16)
    kslc_s[past:past + LANES, :] = new_slc[:, :KV_W]
    vslc_s[past:past + LANES, :] = new_slc[:, KV_W:]
    k_cmp, v_cmp = _compress(kcr_s, vcr_s, pe_ref, w1_ref, w2_ref, a_ref, bm_ref, bp_ref)

    lane8 = lax.broadcasted_iota(jnp.int32, (dec, LANES), 1)
    lane = lax.broadcasted_iota(jnp.int32, (n_rows, LANES), 1)
    rowi = lax.broadcasted_iota(jnp.int32, (n_rows, LANES), 0)
    qpos = past + rowi % dec
    qs = jnp.concatenate([_group_query(q_ref, h, lane8) for h in range(NSA_HEADS)], axis=0).astype(BF16)

    cmask = (CMP_STRIDE * lane + CMP_BLOCK - 1) <= qpos
    p = _masked_softmax(_dot_nt(qs, k_cmp.astype(BF16)) * ATT_SCALE, cmask)
    o_cmp = _dot(p.astype(BF16), v_cmp.astype(BF16))

    sels = []
    qpos8 = past + lax.broadcasted_iota(jnp.int32, (dec, LANES), 0)
    for grp in range(NSA_KV_HEADS):
        base = grp * NSA_GROUP * dec
        psum = p[base:base + dec]
        for r in range(1, NSA_GROUP):
            psum = psum + p[base + r * dec:base + (r + 1) * dec]
        score, allowed = _block_scores(psum, qpos8, lane8)
        cnt = _rank_count(score, lane8, jnp.where(score < FORCE_SCORE, 1.0, 0.0))
        sel = jnp.where(allowed & (cnt < TOP_N), 1.0, 0.0)
        sels += [sel] * NSA_GROUP
    sel = jnp.concatenate(sels, axis=0).astype(BF16)
    n_keys = past + LANES
    kpos = lax.broadcasted_iota(jnp.int32, (n_rows, n_keys), 1)
    qpos_k = past + lax.broadcasted_iota(jnp.int32, (n_rows, n_keys), 0) % dec
    kmask = ((_dot(sel, exp_ref[...]) > 0.5) | (kpos >= past)) & (kpos <= qpos_k)
    p = _masked_softmax(_dot_nt(qs, kslc_s[...]) * ATT_SCALE, kmask)
    o_slc = _dot(p.astype(BF16), vslc_s[...])

    wb = wbuf_ref.shape[1]
    buf = wbuf_ref[0]
    new = win_ref[...]
    zpad = jnp.zeros((tail, KV_W), F32)
    k_win = jnp.concatenate([buf[:, :KV_W], new[:, :KV_W], zpad], axis=0).astype(BF16)
    v_win = jnp.concatenate([buf[:, KV_W:], new[:, KV_W:], zpad], axis=0).astype(BF16)
    n_w = wb + LANES
    kidx = lax.broadcasted_iota(jnp.int32, (n_rows, n_w), 1)
    kpos_w = past - wb + kidx
    d = past + lax.broadcasted_iota(jnp.int32, (n_rows, n_w), 0) % dec - kpos_w
    wmask = (d >= 0) & (d <= WINDOW) & (kpos_w >= 0)
    p = _masked_softmax(_dot_nt(qs, k_win) * ATT_SCALE, wmask)
    o_win = _dot(p.astype(BF16), v_win)
    wout_ref[0, :wb - dec, :] = buf[dec:, :]
    wout_ref[0, wb - dec:, :] = new

    g = _sigmoid(ng_ref[...])
    left = lane8 < NSA_HD
    for pair in range(NSA_HEADS // 2):
        tile = jnp.zeros((dec, LANES), F32)
        for br, o_b in enumerate((o_cmp, o_slc, o_win)):
            pa = _place(o_b[(2 * pair) * dec:(2 * pair + 1) * dec], 2 * pair, lane8)
            pb = _place(o_b[(2 * pair + 1) * dec:(2 * pair + 2) * dec], 2 * pair + 1, lane8)
            tile = tile + _gate_tile(g, br, pair, left) * jnp.where(left, pa, pb)
        o_ref[:, pair * LANES:(pair + 1) * LANES] = tile


def _nsa_sample(q, p, slc_new, win_new, pool_cmp, pool_slc, win_buf, page_table, layer, cw, batch, dec):
    m = q.shape[0]
    n_pages = page_table.shape[1]
    past = n_pages * PAGE_SIZE
    n_pool = pool_cmp.shape[0] // 4
    wb = win_buf.shape[1]
    assert past == CMP_STRIDE * N_CMP_SLOTS and past % SLC_BLOCK == 0 and dec % 8 == 0 and dec <= SLC_BLOCK
    assert (past + dec - CMP_BLOCK) // CMP_STRIDE + 1 == N_CMP_SLOTS - 1 and wb % 8 == 0
    pe, w1, w2, tabs = cw
    full = lambda a: pl.BlockSpec(a.shape, lambda b, pt, nd=a.ndim: (0,) * nd)
    expand = _select_expand(past + LANES)
    page = lambda g: pl.BlockSpec((1, PAGE_SIZE, 2 * KV_W),
                                  lambda b, pt, g=g: (layer * n_pool + pt[b * n_pages + g], 0, 0))
    pages = [page(g) for g in range(n_pages)]
    rows = lambda w, c=0: pl.BlockSpec((dec, w), lambda b, pt, c=c: (b, c))
    grid_spec = pltpu.PrefetchScalarGridSpec(
        num_scalar_prefetch=1,
        grid=(batch,),
        in_specs=pages + pages
        + [pl.BlockSpec((1, wb, 2 * KV_W), lambda b, pt: (layer * batch + b, 0, 0)),
           rows(MIX_W), rows(LANES, C_NG // LANES), rows(2 * KV_W), rows(2 * KV_W),
           full(pe), full(w1), full(w2), full(tabs[0]), full(tabs[1]), full(tabs[2]), full(expand)],
        out_specs=[rows(MIX_W), pl.BlockSpec((1, wb, 2 * KV_W), lambda b, pt: (b, 0, 0))],
        scratch_shapes=[pltpu.VMEM((past, KV_W), F32), pltpu.VMEM((past, KV_W), F32),
                        pltpu.VMEM((past + LANES, KV_W), BF16), pltpu.VMEM((past + LANES, KV_W), BF16)],
    )
    return pl.pallas_call(
        functools.partial(_nsa_sample_kernel, past=past, dec=dec, n_pages=n_pages),
        grid_spec=grid_spec,
        out_shape=[jax.ShapeDtypeStruct((m, MIX_W), F32), jax.ShapeDtypeStruct((batch, wb, 2 * KV_W), F32)],
        compiler_params=_cparams(("parallel",)),
        name="nsa_sample",
    )(page_table.reshape(-1), *([pool_cmp] * n_pages), *([pool_slc] * n_pages), win_buf, q, p, slc_new, win_new,
      pe, w1, w2, *tabs, expand)


def _merge_kernel(x_ref, hg_ref, nsa_ref, ga_ref, gb_ref, wa_ref, wb_ref, wo_ref, o_ref):
    a = _dot(hg_ref[...].astype(BF16), wa_ref[...])
    b = _dot(nsa_ref[...].astype(BF16), wb_ref[...])
    merged = _sigmoid(ga_ref[...]) * a + _sigmoid(gb_ref[...]) * b
    o_ref[...] = x_ref[...] + _dot(merged.astype(BF16), wo_ref[...])


def _merge(x, o_hg, o_nsa, p, w_a, w_b, w_o, tm):
    m = x.shape[0]
    full = lambda a: pl.BlockSpec(a.shape, lambda i: (0, 0))
    return pl.pallas_call(
        _merge_kernel,
        grid=(m // tm,),
        in_specs=[pl.BlockSpec((tm, D_MODEL), lambda i: (i, 0)),
                  pl.BlockSpec((tm, MIX_W), lambda i: (i, 0)),
                  pl.BlockSpec((tm, MIX_W), lambda i: (i, 0)),
                  pl.BlockSpec((tm, D_MODEL), lambda i: (i, C_MG // D_MODEL)),
                  pl.BlockSpec((tm, D_MODEL), lambda i: (i, C_MG // D_MODEL + 1)),
                  full(w_a), full(w_b), full(w_o)],
        out_specs=pl.BlockSpec((tm, D_MODEL), lambda i: (i, 0)),
        out_shape=jax.ShapeDtypeStruct((m, D_MODEL), F32),
        compiler_params=_cparams(("parallel",)),
        name="merge",
    )(x, o_hg, o_nsa, p, p, w_a, w_b, w_o)


def _ffn_kernel(*refs, tm, seq, has_state):
    if has_state:
        x_ref, gn_ref, wg_ref, wu_ref, cw_ref, cb_ref, wd_ref, p1_ref, p2_ref, o_ref, gt_ref, xn_ref, acc_ref = refs
    else:
        x_ref, gn_ref, wg_ref, wu_ref, cw_ref, cb_ref, wd_ref, o_ref, gt_ref, xn_ref, acc_ref, carry_ref = refs
    i, f = pl.program_id(0), pl.program_id(1)

    @pl.when(f == 0)
    def _():
        x = x_ref[...]
        ms = jnp.mean(x * x, axis=-1, keepdims=True)
        xn_ref[...] = (x * lax.rsqrt(ms + NORM_EPS) * gn_ref[...]).astype(BF16)
        acc_ref[...] = jnp.zeros_like(acc_ref)

    xn = xn_ref[...]
    g = _dot(xn, wg_ref[...])
    u = _dot(xn, wu_ref[...])
    row = lax.broadcasted_iota(jnp.int32, g.shape, 0)
    g1 = pltpu.roll(g, 1, 0)
    g2 = pltpu.roll(g, 2, 0)
    if has_state:
        tpos = row % seq
        g1 = jnp.where(tpos == 0, p1_ref[...], g1)
        g2 = jnp.where(tpos < 2, p2_ref[...], g2)
        gt_ref[...] = g
    else:
        first = (i % (seq // tm)) == 0
        prev = jnp.where(first, 0.0, carry_ref[f])
        g1 = jnp.where(row == 0, prev[7:8, :], g1)
        g2 = jnp.where(row == 0, prev[6:7, :], jnp.where(row == 1, prev[7:8, :], g2))
        carry_ref[f] = g[tm - 8:, :]
        gt_ref[0] = g[tm - 8:, :]
    conv = cw_ref[0:1, :] * g2 + cw_ref[1:2, :] * g1 + cw_ref[2:3, :] * g + cb_ref[...]
    acc_ref[...] += _dot((_gelu_tanh(conv) * u).astype(BF16), wd_ref[...])

    @pl.when(f == pl.num_programs(1) - 1)
    def _():
        o_ref[...] = x_ref[...] + acc_ref[...]


def _ffn(x, gn, w_gu, conv_w, conv_b, w_d, prev, seq, tm, tf):
    m = x.shape[0]
    nf = D_FF // tf
    has_state = prev is not None
    in_specs = [pl.BlockSpec((tm, D_MODEL), lambda i, f: (i, 0)),
                pl.BlockSpec((1, D_MODEL), lambda i, f: (0, 0)),
                pl.BlockSpec((D_MODEL, tf), lambda i, f: (0, f)),
                pl.BlockSpec((D_MODEL, tf), lambda i, f: (0, nf + f)),
                pl.BlockSpec((CONV_W, tf), lambda i, f: (0, f)),
                pl.BlockSpec((1, tf), lambda i, f: (0, f)),
                pl.BlockSpec((tf, D_MODEL), lambda i, f: (f, 0))]
    args = [x, gn, w_gu, w_gu, conv_w, conv_b, w_d]
    scratch = [pltpu.VMEM((tm, D_MODEL), BF16), pltpu.VMEM((tm, D_MODEL), F32)]
    if has_state:
        assert tm % seq == 0
        in_specs += [pl.BlockSpec((tm, tf), lambda i, f: (i, f))] * 2
        args += list(prev)
        gt_spec = pl.BlockSpec((tm, tf), lambda i, f: (i, f))
        gt_shape = jax.ShapeDtypeStruct((m, D_FF), F32)
    else:
        assert seq % tm == 0
        gt_spec = pl.BlockSpec((1, 8, tf), lambda i, f: (i, 0, f))
        gt_shape = jax.ShapeDtypeStruct((m // tm, 8, D_FF), F32)
        scratch.append(pltpu.VMEM((nf, 8, tf), F32))
    return pl.pallas_call(
        functools.partial(_ffn_kernel, tm=tm, seq=seq, has_state=has_state),
        grid=(m // tm, nf),
        in_specs=in_specs,
        out_specs=[pl.BlockSpec((tm, D_MODEL), lambda i, f: (i, 0)), gt_spec],
        out_shape=[jax.ShapeDtypeStruct((m, D_MODEL), F32), gt_shape],
        scratch_shapes=scratch,
        compiler_params=_cparams(("arbitrary", "arbitrary")),
        name="ffn",
    )(*args)


def _prepare_weights(w_in, hg_lower, cmp_pe, cmp_w1, cmp_w2, w_branch, w_out, w_gate_up, w_down):
    depth = w_in.shape[0]
    c_rest = 4 * MIX_W
    c_ng = c_rest + MIX_W + 6 * KV_W
    n_gate = 3 * NSA_HEADS
    w_in_p = jnp.concatenate([
        w_in[:, :, :c_rest], w_in[:, :, c_ng + n_gate:], w_in[:, :, c_rest:c_ng], w_in[:, :, c_ng:c_ng + n_gate],
        jnp.zeros((depth, D_MODEL, N_IN - w_in.shape[2]), w_in.dtype)], axis=2).astype(BF16)
    lbs = jax.nn.softmax(hg_lower.astype(F32), axis=0)
    lbs = (jnp.cumsum(lbs, axis=0) - lbs[0:1]).reshape(depth, HG_HEADS, 1, HG_DK)
    def block_diag(w):
        w = w.astype(BF16)
        z = jnp.zeros_like(w)
        return jnp.concatenate([jnp.concatenate([w, z], axis=-1), jnp.concatenate([z, w], axis=-1)], axis=-2)

    pe = jnp.concatenate([cmp_pe] * NSA_KV_HEADS, axis=-1)
    return (w_in_p, lbs, pe, block_diag(cmp_w1), block_diag(cmp_w2), w_branch.astype(BF16), w_out.astype(BF16),
            w_gate_up.astype(BF16), w_down.astype(BF16))


def _trunk(x, pos, caches, page_table, weights, batch, seq, cfg):
    (norm_mix, w_in_p, lbs, hg_norm, pe, w1, w2, w_br, w_out, norm_ffn, w_gu, conv_w, conv_b, w_down, norm_final) = weights
    depth = w_in_p.shape[0]
    tm = cfg["tm"]
    tables = _rope_tables(pos)
    if seq < tm:
        tables = tuple(jnp.tile(t, (tm // seq, 1)) for t in tables)
    cmp_tabs = _rope_tables(jnp.arange(N_CMP_SLOTS, dtype=jnp.int32) * CMP_STRIDE + CMP_BLOCK - 1)
    outs = [[] for _ in range(5)]
    for l in range(depth):
        p = _norm_matmul(x, norm_mix[l][None], w_in_p[l], cfg["nm_tm"], cfg["tn"])
        q_rot, cmp_rows, slc_rows, win_rows, kvb = _prep(p, tables, tm)
        s0 = None if caches is None else caches[3][l]
        o_hg, s_hg = _gla(p, lbs[l], hg_norm[l][None], s0, batch, seq, cfg["gla_rows"], cfg["chunk"], cfg["sub"],
                          cfg["gla_mm"], cfg["gla_heads"])
        cw = (pe[l], w1[l], w2[l], cmp_tabs)
        if caches is None:
            o_nsa = _nsa_prompt(q_rot, p, cmp_rows, kvb, cw, batch, seq, LANES)
            new_win = win_rows.reshape(batch, seq, 2 * KV_W)[:, -min(WINDOW, seq):]
        else:
            o_nsa, new_win = _nsa_sample(q_rot, p, slc_rows, win_rows, caches[0], caches[1], caches[2], page_table, l,
                                         cw, batch, seq)
        x = _merge(x, o_hg, o_nsa, p, w_br[l, 0], w_br[l, 1], w_out[l], tm)
        if caches is None:
            x, g_tail = _ffn(x, norm_ffn[l][None], w_gu[l], conv_w[l], conv_b[l][None], w_down[l], None, seq,
                             cfg["ffn_tm"], cfg["tf"])
            per = seq // cfg["ffn_tm"]
            c_st = g_tail[per - 1::per, 8 - (CONV_W - 1):, :]
        else:
            st = caches[4][l]
            prev2 = jnp.pad(st, ((0, 0), (0, seq - (CONV_W - 1)), (0, 0))).reshape(batch * seq, D_FF)
            prev1 = jnp.pad(st[:, 1:], ((0, 0), (0, seq - 1), (0, 0))).reshape(batch * seq, D_FF)
            x, g_all = _ffn(x, norm_ffn[l][None], w_gu[l], conv_w[l], conv_b[l][None], w_down[l], (prev1, prev2), seq,
                            cfg["ffn_tm"], cfg["tf"])
            c_st = g_all.reshape(batch, seq, D_FF)[:, seq - (CONV_W - 1):]
        kv6 = lambda a: a.reshape(batch, -1, 2, NSA_KV_HEADS, NSA_HD)
        outs[0].append(kv6(cmp_rows))
        outs[1].append(kv6(slc_rows))
        outs[2].append(kv6(new_win))
        outs[3].append(s_hg)
        outs[4].append(c_st)
    y = _rmsnorm(x, norm_final[None], tm).reshape(batch, seq, D_MODEL)
    return (y,) + tuple(jnp.stack(o) for o in outs)


def kernel(x_prompt, x_sample, cache_cmp_kv, cache_slc_kv, cache_win_kv, state_hgrn, state_conv, page_table, norm_mix, w_in, hg_lower, hg_norm, cmp_pe, cmp_w1, cmp_w2, w_branch, w_out, norm_ffn, w_gate_up, conv_w, conv_b, w_down, norm_final):
    (w_in_p, lbs, pe, w1, w2, w_br, w_o, w_gu, w_d) = _prepare_weights(
        w_in, hg_lower, cmp_pe, cmp_w1, cmp_w2, w_branch, w_out, w_gate_up, w_down)
    weights = (norm_mix, w_in_p, lbs, hg_norm, pe, w1, w2, w_br, w_o, norm_ffn, w_gu, conv_w, conv_b, w_d, norm_final)
    b_p, l_p, _ = x_prompt.shape
    b_s, l_s, _ = x_sample.shape
    depth, n_pool = cache_cmp_kv.shape[:2]
    past = page_table.shape[1] * PAGE_SIZE
    pos_p = jnp.arange(l_p, dtype=jnp.int32)
    pos_s = past + jnp.arange(l_s, dtype=jnp.int32)
    cfg_p = dict(tm=1024, nm_tm=1024, tn=1408, gla_rows=256, chunk=64, sub=16, gla_mm=BF16, gla_heads=1, ffn_tm=512,
                 tf=1408)
    cfg_s = dict(tm=256, nm_tm=1024, tn=1408, gla_rows=l_s, chunk=l_s, sub=l_s, gla_mm=F32, gla_heads=HG_HEADS,
                 ffn_tm=256, tf=1408)
    res_p = _trunk(x_prompt.reshape(b_p * l_p, D_MODEL), pos_p, None, None, weights, b_p, l_p, cfg_p)
    caches = (cache_cmp_kv.reshape(depth * n_pool, PAGE_SIZE, 2 * KV_W),
              cache_slc_kv.reshape(depth * n_pool, PAGE_SIZE, 2 * KV_W),
              cache_win_kv.reshape(depth * b_s, -1, 2 * KV_W), state_hgrn, state_conv)
    res_s = _trunk(x_sample.reshape(b_s * l_s, D_MODEL), pos_s, caches, page_table, weights, b_s, l_s, cfg_s)
    out = []
    for a, b in zip(res_p, res_s):
        out += [a, b]
    out[7] = out[7].reshape(depth, b_s, -1, 2, NSA_KV_HEADS, NSA_HD)
    return tuple(out)
```

```python
import functools

import jax
import jax.numpy as jnp
from jax import lax
from jax.experimental import pallas as pl
from jax.experimental.pallas import tpu as pltpu

F32 = jnp.float32
BF16 = jnp.bfloat16

D_MODEL = 1024
MIX_W = 512
HG_DK = 128
HG_HEADS = 4
HG_DV = 128
NSA_HD = 64
NSA_HEADS = 8
NSA_KV_HEADS = 2
NSA_GROUP = 4
KV_W = NSA_KV_HEADS * NSA_HD
CMP_BLOCK = 32
CMP_STRIDE = 16
SLC_BLOCK = 64
TOP_N = 16
WINDOW = 512
ROPE_THETA = 500000.0
ROPE_DIM = 16
D_FF = 2816
CONV_W = 3
PAGE_SIZE = 128
NORM_EPS = 1e-6
NEG_INF = -1e30
LB_FLOOR = 1e-30
FORCE_SCORE = 1e9
ATT_SCALE = NSA_HD ** -0.5
LOG2E = 1.4426950408889634

LANES = 128
N_CMP_SLOTS = 128
SLC_CHUNK = 512
VMEM_LIMIT = 56 * 1024 * 1024

C_HQ, C_HF, C_HI, C_HG = 0, 512, 1024, 1536
C_MG = 2048
C_NQ = 4096
C_KV = 4608
C_NG = 5376
N_IN = 5632


def _cparams(sem):
    return pltpu.CompilerParams(dimension_semantics=sem, vmem_limit_bytes=VMEM_LIMIT)


def _sigmoid(x):
    return 1.0 / (1.0 + jnp.exp(-x))


def _gelu_tanh(x):
    return 0.5 * x * (1.0 + jnp.tanh(0.7978845608028654 * (x + 0.044715 * (x * x * x))))


def _dot(a, b):
    return jnp.dot(a, b, preferred_element_type=F32)


def _dot_nt(a, b):
    return lax.dot_general(a, b, (((1,), (1,)), ((), ())), preferred_element_type=F32)


def _masked_softmax(s, mask):
    s = jnp.where(mask, s, NEG_INF)
    m = jnp.max(s, axis=-1, keepdims=True)
    e = jnp.where(mask, jnp.exp(s - m), 0.0)
    return e * (1.0 / jnp.maximum(jnp.sum(e, axis=-1, keepdims=True), 1e-30))


def _norm_matmul_kernel(x_ref, g_ref, w_ref, o_ref, xn_ref):
    @pl.when(pl.program_id(1) == 0)
    def _():
        x = x_ref[...]
        ms = jnp.mean(x * x, axis=-1, keepdims=True)
        xn_ref[...] = (x * lax.rsqrt(ms + NORM_EPS) * g_ref[...]).astype(BF16)

    o_ref[...] = _dot(xn_ref[...], w_ref[...])


def _norm_matmul(x, g, w, tm, tn):
    m, k = x.shape
    n = w.shape[1]
    return pl.pallas_call(
        _norm_matmul_kernel,
        grid=(m // tm, n // tn),
        in_specs=[pl.BlockSpec((tm, k), lambda i, j: (i, 0)),
                  pl.BlockSpec((1, k), lambda i, j: (0, 0)),
                  pl.BlockSpec((k, tn), lambda i, j: (0, j))],
        out_specs=pl.BlockSpec((tm, tn), lambda i, j: (i, j)),
        out_shape=jax.ShapeDtypeStruct((m, n), F32),
        scratch_shapes=[pltpu.VMEM((tm, k), BF16)],
        compiler_params=_cparams(("parallel", "arbitrary")),
        name="norm_matmul",
    )(x, g, w)


def _rmsnorm_kernel(x_ref, g_ref, o_ref):
    x = x_ref[...]
    ms = jnp.mean(x * x, axis=-1, keepdims=True)
    o_ref[...] = x * lax.rsqrt(ms + NORM_EPS) * g_ref[...]


def _rmsnorm(x, g, tm):
    m, k = x.shape
    return pl.pallas_call(
        _rmsnorm_kernel,
        grid=(m // tm,),
        in_specs=[pl.BlockSpec((tm, k), lambda i: (i, 0)), pl.BlockSpec((1, k), lambda i: (0, 0))],
        out_specs=pl.BlockSpec((tm, k), lambda i: (i, 0)),
        out_shape=jax.ShapeDtypeStruct((m, k), F32),
        compiler_params=_cparams(("parallel",)),
        name="final_norm",
    )(x, g)


def _rope_tables(pos):
    half = ROPE_DIM // 2
    inv = jnp.float32(ROPE_THETA) ** (-jnp.arange(half, dtype=F32) * 2.0 / ROPE_DIM)
    ang = pos.astype(F32)[:, None] * inv[None, :]
    cos, sin = jnp.cos(ang), jnp.sin(ang)
    t = pos.shape[0]
    z8 = jnp.zeros((t, half), F32)
    rest = NSA_HD - ROPE_DIM
    a = jnp.concatenate([cos, cos, jnp.ones((t, rest), F32)], axis=1)
    bm = jnp.concatenate([-sin, z8, jnp.zeros((t, rest), F32)], axis=1)
    bp = jnp.concatenate([z8, sin, jnp.zeros((t, rest), F32)], axis=1)
    return tuple(jnp.concatenate([v, v], axis=1) for v in (a, bm, bp))


def _rope(x, a, bm, bp):
    half = ROPE_DIM // 2
    return x * a + pltpu.roll(x, LANES - half, 1) * bm + pltpu.roll(x, half, 1) * bp


def _prep_kernel(nq_ref, kc_ref, vc_ref, ks_ref, vs_ref, kw_ref, vw_ref, a_ref, bm_ref, bp_ref,
                 q_ref, cmp_ref, slc_ref, win_ref, kvb_ref):
    a, bm, bp = a_ref[...], bm_ref[...], bp_ref[...]
    for c in range(MIX_W // LANES):
        q_ref[:, c * LANES:(c + 1) * LANES] = _rope(nq_ref[:, c * LANES:(c + 1) * LANES], a, bm, bp)
    cmp_ref[:, :KV_W] = kc_ref[...]
    cmp_ref[:, KV_W:] = vc_ref[...]
    ks = _rope(ks_ref[...], a, bm, bp)
    vs = vs_ref[...]
    slc_ref[:, :KV_W] = ks
    slc_ref[:, KV_W:] = vs
    kw = _rope(kw_ref[...], a, bm, bp)
    vw = vw_ref[...]
    win_ref[:, :KV_W] = kw
    win_ref[:, KV_W:] = vw
    kvb_ref[:, 0 * KV_W:1 * KV_W] = ks.astype(BF16)
    kvb_ref[:, 1 * KV_W:2 * KV_W] = vs.astype(BF16)
    kvb_ref[:, 2 * KV_W:3 * KV_W] = kw.astype(BF16)
    kvb_ref[:, 3 * KV_W:4 * KV_W] = vw.astype(BF16)


def _prep(p, tables, tm):
    m = p.shape[0]
    nt = tables[0].shape[0] // tm
    kvb0 = C_KV // KV_W
    col = lambda c: pl.BlockSpec((tm, KV_W), lambda i, c=c: (i, kvb0 + c))
    tab = pl.BlockSpec((tm, LANES), lambda i: (i % nt, 0))
    row = lambda w: pl.BlockSpec((tm, w), lambda i: (i, 0))
    return pl.pallas_call(
        _prep_kernel,
        grid=(m // tm,),
        in_specs=[pl.BlockSpec((tm, MIX_W), lambda i: (i, C_NQ // MIX_W))] + [col(c) for c in range(6)] + [tab] * 3,
        out_specs=[row(MIX_W), row(2 * KV_W), row(2 * KV_W), row(2 * KV_W), row(4 * KV_W)],
        out_shape=[jax.ShapeDtypeStruct((m, MIX_W), F32)] + [jax.ShapeDtypeStruct((m, 2 * KV_W), F32)] * 3
        + [jax.ShapeDtypeStruct((m, 4 * KV_W), BF16)],
        compiler_params=_cparams(("parallel",)),
        name="prep",
    )(p, p, p, p, p, p, p, *tables)


def _gla_kernel(*refs, chunk, sub, n_chunks, has_s0, mm_dtype, heads):
    if has_s0:
        q_ref, f_ref, i_ref, g_ref, lb_ref, ng_ref, s0_ref, o_ref, so_ref, st_ref, b_s, k_s = refs
    else:
        q_ref, f_ref, i_ref, g_ref, lb_ref, ng_ref, o_ref, so_ref, st_ref, b_s, k_s = refs
        s0_ref = None
    t = pl.program_id(2)

    @pl.when(t == 0)
    def _():
        for hh in range(heads):
            st_ref[hh] = jnp.zeros((HG_DV, HG_DK), F32) if s0_ref is None else s0_ref[0, hh].T

    for hh in range(heads):
        cols = slice(hh * HG_DK, (hh + 1) * HG_DK)
        _gla_head(q_ref.at[:, cols], f_ref.at[:, cols], i_ref.at[:, cols], g_ref.at[:, cols], lb_ref[hh], ng_ref[...],
                  o_ref.at[:, cols], st_ref.at[hh], b_s.at[hh], k_s.at[hh],
                  chunk=chunk, sub=sub, n_chunks=n_chunks, mm_dtype=mm_dtype)

    @pl.when(t == pl.num_programs(2) - 1)
    def _():
        for hh in range(heads):
            so_ref[0, hh] = st_ref[hh].T


def _gla_head(q_ref, f_ref, i_ref, g_ref, lb, ng, o_ref, st_ref, b_all, k_all, *, chunk, sub, n_chunks, mm_dtype):
    log_lb = jnp.log(jnp.maximum(lb, LB_FLOOR))
    log_1m = jnp.log1p(-lb)
    one_m = 1.0 - lb
    row = lax.broadcasted_iota(jnp.int32, (chunk, HG_DK), 0)
    row_s = lax.broadcasted_iota(jnp.int32, (sub, HG_DK), 0)
    lane_s = lax.broadcasted_iota(jnp.int32, (sub, LANES), 1)
    pad = LANES - chunk

    st = st_ref[...]
    for c in range(n_chunks):
        r0 = c * chunk
        b_s, k_s = b_all.at[c], k_all.at[c]
        fr = f_ref[pl.ds(r0, chunk), :]
        qr = q_ref[pl.ds(r0, chunk), :]
        v = i_ref[pl.ds(r0, chunk), :]
        log_sig = jnp.minimum(fr, 0.0) - jnp.log1p(jnp.exp(-jnp.abs(fr)))
        bb = log_1m + log_sig
        log_f = jnp.maximum(log_lb, bb) + jnp.log1p(jnp.exp(-jnp.abs(log_lb - bb)))
        k = one_m / (1.0 + jnp.exp(fr))
        q = qr * _sigmoid(qr) * (HG_DK ** -0.5)
        b = log_f
        sh = 1
        while sh < chunk:
            b = b + jnp.where(row >= sh, pltpu.roll(b, sh, 0), 0.0)
            sh *= 2
        b_s[...] = b
        k_s[...] = k
        o = _dot_nt((q * jnp.exp(b)).astype(mm_dtype), st.astype(mm_dtype))
        pad_rows = lambda a: jnp.concatenate([a, jnp.zeros((pad, HG_DK), F32)], axis=0) if pad > 0 else a
        v_p = pad_rows(v)
        parts = []
        for blk in range(chunk // sub):
            lo = blk * sub
            b_i = b[lo:lo + sub]
            q_i = q[lo:lo + sub]
            if blk > 0:
                ref_b = b_s[lo - 1:lo, :]
                q_d = q_i * jnp.exp(b_i - ref_b)
                k_d = jnp.where(row < lo, k * jnp.exp(jnp.minimum(ref_b - b, 0.0)), 0.0)
                att = _dot_nt(q_d.astype(mm_dtype), pad_rows(k_d).astype(mm_dtype))
            else:
                att = jnp.zeros((sub, LANES), F32)
            zs = []
            for s in range(sub):
                e = jnp.exp(jnp.where(row_s >= s, b_i - b_s[lo + s:lo + s + 1, :], NEG_INF))
                zs.append((q_i * e * k_s[lo + s:lo + s + 1, :]).astype(mm_dtype))
            red = _dot(jnp.concatenate(zs, axis=0), jnp.ones((HG_DK, LANES), mm_dtype))
            for s in range(sub):
                att = jnp.where(lane_s == lo + s, red[s * sub:(s + 1) * sub], att)
            parts.append(att)
        att = parts[0] if len(parts) == 1 else jnp.concatenate(parts, axis=0)
        o = o + _dot(att.astype(mm_dtype), v_p.astype(mm_dtype))
        b_last = b_s[chunk - 1:chunk, :]
        k_dec = pad_rows(k * jnp.exp(b_last - b))
        st = jnp.exp(b_last) * st + _dot(v_p.T.astype(mm_dtype), k_dec.astype(mm_dtype))
        gr = g_ref[pl.ds(r0, chunk), :]
        ms = jnp.mean(o * o, axis=-1, keepdims=True)
        o_ref[pl.ds(r0, chunk), :] = o * lax.rsqrt(ms + NORM_EPS) * ng * (gr * _sigmoid(gr))
    st_ref[...] = st


def _gla(p, lb, ng, s0, batch, seq, rows, chunk, sub, mm_dtype, heads):
    m = p.shape[0]
    nt = seq // rows
    has_s0 = s0 is not None
    w = heads * HG_DK
    col = lambda c0: pl.BlockSpec((rows, w), lambda b, h, t, c0=c0: (b * nt + t, c0 // w + h))
    in_specs = [col(C_HQ), col(C_HF), col(C_HI), col(C_HG),
                pl.BlockSpec((heads, 1, HG_DK), lambda b, h, t: (h, 0, 0)),
                pl.BlockSpec((1, HG_DV), lambda b, h, t: (0, 0))]
    args = [p, p, p, p, lb, ng]
    if has_s0:
        in_specs.append(pl.BlockSpec((1, heads, HG_DK, HG_DV), lambda b, h, t: (b, h, 0, 0)))
        args.append(s0)
    return pl.pallas_call(
        functools.partial(_gla_kernel, chunk=chunk, sub=sub, n_chunks=rows // chunk, has_s0=has_s0, mm_dtype=mm_dtype,
                          heads=heads),
        grid=(batch, HG_HEADS // heads, nt),
        in_specs=in_specs,
        out_specs=[pl.BlockSpec((rows, w), lambda b, h, t: (b * nt + t, h)),
                   pl.BlockSpec((1, heads, HG_DK, HG_DV), lambda b, h, t: (b, h, 0, 0))],
        out_shape=[jax.ShapeDtypeStruct((m, MIX_W), F32),
                   jax.ShapeDtypeStruct((batch, HG_HEADS, HG_DK, HG_DV), F32)],
        scratch_shapes=[pltpu.VMEM((heads, HG_DV, HG_DK), F32),
                        pltpu.VMEM((heads, rows // chunk, chunk, HG_DK), F32),
                        pltpu.VMEM((heads, rows // chunk, chunk, HG_DK), F32)],
        compiler_params=_cparams(("parallel", "parallel", "arbitrary")),
        name="gla",
    )(*args)


def _compress_one(row_of_chunks, kv, pe_ref, w1_ref, w2_ref):
    n = N_CMP_SLOTS
    acc_a = jnp.zeros((n, KV_W), F32)
    acc_b = jnp.zeros((n, KV_W), F32)
    for j in range(CMP_STRIDE):
        x = row_of_chunks(j)
        acc_a = acc_a + _dot((x + pe_ref[kv, j:j + 1, :]).astype(BF16), w1_ref[kv, j])
        acc_b = acc_b + _dot((x + pe_ref[kv, CMP_STRIDE + j:CMP_STRIDE + j + 1, :]).astype(BF16),
                             w1_ref[kv, CMP_STRIDE + j])
    h = _gelu_tanh(acc_a + pltpu.roll(acc_b, n - 1, 0))
    out = _dot(h.astype(BF16), w2_ref[kv])
    slot = lax.broadcasted_iota(jnp.int32, (n, KV_W), 0)
    return jnp.where(slot < n - 1, out, 0.0)


def _compress(k_rows, v_rows, pe_ref, w1_ref, w2_ref, a_ref, bm_ref, bp_ref):
    k_cmp = _rope(_compress_one(k_rows, 0, pe_ref, w1_ref, w2_ref), a_ref[...], bm_ref[...], bp_ref[...])
    return k_cmp, _compress_one(v_rows, 1, pe_ref, w1_ref, w2_ref)


def _strided_rows(rows_ref):
    return lambda j: rows_ref[pl.ds(j, N_CMP_SLOTS, stride=CMP_STRIDE), :]


def _paged_rows(pages, kv):
    lanes = slice(kv * KV_W, (kv + 1) * KV_W)
    return lambda j: jnp.concatenate([pg[0, j, :, lanes] for pg in pages], axis=0)


def _group_query(q_ref, head, lane):
    grp, pair, half = head // NSA_GROUP, head // 2, head % 2
    x = q_ref[:, pair * LANES:(pair + 1) * LANES]
    if half != grp:
        x = pltpu.roll(x, NSA_HD, 1)
    return jnp.where((lane >= grp * NSA_HD) & (lane < (grp + 1) * NSA_HD), x, 0.0)


def _block_scores(psum, qpos, lane):
    w = psum
    for k in range(1, CMP_BLOCK // CMP_STRIDE + SLC_BLOCK // CMP_STRIDE - 1):
        w = w + jnp.where(lane >= k, pltpu.roll(psum, k, 1), 0.0)
    blk = lane // 4
    cur = qpos // SLC_BLOCK
    forced = (blk == 0) | (blk == cur) | (blk == cur - 1)
    allowed = (blk * SLC_BLOCK <= qpos) & (lane % 4 == 3)
    return jnp.where(allowed, jnp.where(forced, FORCE_SCORE, w), NEG_INF), allowed


def _rank_count(score, lane, cnt):
    for s in range(LANES // 4):
        c = 4 * s + 3
        col = score[:, c:c + 1]
        cnt = cnt + jnp.where((col > score) | ((col == score) & (lane > c)), 1.0, 0.0)
    return cnt


def _place(res, head, lane):
    grp, half = head // NSA_GROUP, head % 2
    return res if half == grp else pltpu.roll(res, NSA_HD, 1)


def _gate_tile(g, branch, pair, left):
    c = branch * NSA_HEADS + 2 * pair
    return jnp.where(left, g[:, c:c + 1], g[:, c + 1:c + 2])


def _nsa_prompt_kernel(q_ref, ng_ref, kcr_ref, vcr_ref, kvb_ref, pe_ref, w1_ref, w2_ref, a_ref, bm_ref, bp_ref, exp_ref,
                       o_ref, kc_s, vc_s, wt_s, sc_s, sel_s, m_s, l_s, acc_s, *, tq, seq):
    i = pl.program_id(1)

    @pl.when(i == 0)
    def _():
        k_cmp, v_cmp = _compress(_strided_rows(kcr_ref), _strided_rows(vcr_ref), pe_ref, w1_ref, w2_ref,
                                 a_ref, bm_ref, bp_ref)
        kc_s[...] = k_cmp.astype(BF16)
        vc_s[...] = v_cmp.astype(BF16)

    q0 = i * tq
    lane = lax.broadcasted_iota(jnp.int32, (tq, LANES), 1)
    left = lane < NSA_HD
    qpos = q0 + lax.broadcasted_iota(jnp.int32, (tq, LANES), 0)
    g = _sigmoid(ng_ref[...])
    qm = [(_group_query(q_ref, h, lane) * (ATT_SCALE * LOG2E)).astype(BF16) for h in range(NSA_HEADS)]
    qg = [jnp.concatenate(qm[grp * NSA_GROUP:(grp + 1) * NSA_GROUP], axis=0) for grp in range(NSA_KV_HEADS)]
    rows_of = lambda a, r: a[r * tq:(r + 1) * tq]
    kc, vc = kc_s[...], vc_s[...]
    cmask = (CMP_STRIDE * lane + CMP_BLOCK - 1) <= qpos
    n_blk = seq // SLC_BLOCK
    blk_t = lax.broadcasted_iota(jnp.int32, (n_blk, tq), 0)
    qpos_t = q0 + lax.broadcasted_iota(jnp.int32, (n_blk, tq), 1)
    cur_t = qpos_t // SLC_BLOCK
    forced_t = (blk_t == 0) | (blk_t == cur_t) | (blk_t == cur_t - 1)
    allowed_t = blk_t * SLC_BLOCK <= qpos_t

    o_cmp, o_slc, o_win = [None] * NSA_HEADS, [None] * NSA_HEADS, [None] * NSA_HEADS
    for grp in range(NSA_KV_HEADS):
        psum = jnp.zeros((tq, LANES), F32)
        s_all = _dot_nt(qg[grp], kc)
        ps = []
        for r in range(NSA_GROUP):
            s = jnp.where(cmask, rows_of(s_all, r), NEG_INF)
            e = jnp.where(cmask, jnp.exp2(s - jnp.max(s, axis=-1, keepdims=True)), 0.0)
            p = e * (1.0 / jnp.maximum(jnp.sum(e, axis=-1, keepdims=True), 1e-30))
            psum = psum + p
            ps.append(p.astype(BF16))
        o_all = _dot(jnp.concatenate(ps, axis=0), vc)
        for r in range(NSA_GROUP):
            o_cmp[grp * NSA_GROUP + r] = _place(rows_of(o_all, r), grp * NSA_GROUP + r, lane)
        w = psum
        for k in range(1, CMP_BLOCK // CMP_STRIDE + SLC_BLOCK // CMP_STRIDE - 1):
            w = w + jnp.where(lane >= k, pltpu.roll(psum, k, 1), 0.0)
        wt_s[...] = w.T
        imp_t = wt_s[pl.ds(SLC_BLOCK // CMP_STRIDE - 1, n_blk, stride=SLC_BLOCK // CMP_STRIDE), :]
        score = jnp.where(allowed_t, jnp.where(forced_t, FORCE_SCORE, imp_t), NEG_INF)
        sc_s[...] = score
        cnt = jnp.zeros((n_blk, tq), F32)
        for c in range(n_blk):
            other = sc_s[c:c + 1, :]
            cnt = cnt + jnp.where((other > score) | ((other == score) & (blk_t > c)), 1.0, 0.0)
        sel_t = jnp.where(allowed_t & (cnt < TOP_N), 1.0, 0.0)
        sel_s[grp] = jnp.concatenate([sel_t, jnp.zeros((LANES - n_blk, tq), F32)], axis=0).T.astype(BF16)

    for h in range(NSA_HEADS):
        m_s[h] = jnp.full((tq, LANES), NEG_INF, F32)
        l_s[h] = jnp.zeros((tq, LANES), F32)
        acc_s[h] = jnp.zeros((tq, LANES), F32)
    kc_w = SLC_CHUNK
    for ck in range(seq // kc_w):
        @pl.when(ck * kc_w <= q0)
        def _(ck=ck):
            kpos = ck * kc_w + lax.broadcasted_iota(jnp.int32, (tq, kc_w), 1)
            causal = kpos <= q0 + lax.broadcasted_iota(jnp.int32, (tq, kc_w), 0)
            k_c = kvb_ref[ck * kc_w:(ck + 1) * kc_w, 0 * KV_W:1 * KV_W]
            v_c = kvb_ref[ck * kc_w:(ck + 1) * kc_w, 1 * KV_W:2 * KV_W]
            for grp in range(NSA_KV_HEADS):
                picked = _dot(sel_s[grp], exp_ref[:, ck * kc_w:(ck + 1) * kc_w]) > 0.5
                bias = jnp.where(picked & causal, 0.0, NEG_INF)
                s_all = _dot_nt(qg[grp], k_c)
                ps, alphas = [], []
                for r in range(NSA_GROUP):
                    h = grp * NSA_GROUP + r
                    s = rows_of(s_all, r) + bias
                    m_prev = m_s[h]
                    m_new = jnp.maximum(m_prev, jnp.max(s, axis=-1, keepdims=True))
                    alpha = jnp.exp2(m_prev - m_new)
                    p = jnp.exp2(s - jnp.concatenate([m_new] * (kc_w // LANES), axis=1))
                    l_s[h] = alpha * l_s[h] + jnp.sum(p, axis=-1, keepdims=True)
                    m_s[h] = m_new
                    ps.append(p.astype(BF16))
                    alphas.append(alpha)
                pv = _dot(jnp.concatenate(ps, axis=0), v_c)
                for r in range(NSA_GROUP):
                    h = grp * NSA_GROUP + r
                    acc_s[h] = alphas[r] * acc_s[h] + rows_of(pv, r)

    n_win = WINDOW + tq
    w0 = pl.multiple_of(jnp.maximum(q0 - WINDOW, 0), tq)
    dwin = (q0 - w0) + lax.broadcasted_iota(jnp.int32, (tq, n_win), 0) - lax.broadcasted_iota(jnp.int32, (tq, n_win), 1)
    wbias = jnp.where((dwin >= 0) & (dwin <= WINDOW), 0.0, NEG_INF)
    k_win = kvb_ref[pl.ds(w0, n_win), 2 * KV_W:3 * KV_W]
    v_win = kvb_ref[pl.ds(w0, n_win), 3 * KV_W:4 * KV_W]
    for grp in range(NSA_KV_HEADS):
        s_all = _dot_nt(qg[grp], k_win)
        ps, inv = [], []
        for r in range(NSA_GROUP):
            s = rows_of(s_all, r) + wbias
            p = jnp.exp2(s - jnp.max(s, axis=-1, keepdims=True))
            inv.append(1.0 / jnp.sum(p, axis=-1, keepdims=True))
            ps.append(p.astype(BF16))
        o_all = _dot(jnp.concatenate(ps, axis=0), v_win)
        for r in range(NSA_GROUP):
            h = grp * NSA_GROUP + r
            o_win[h] = _place(rows_of(o_all, r) * inv[r], h, lane)
            o_slc[h] = _place(acc_s[h] * (1.0 / l_s[h]), h, lane)
    for pair in range(NSA_HEADS // 2):
        ha, hb = 2 * pair, 2 * pair + 1
        o_ref[:, pair * LANES:(pair + 1) * LANES] = (
            _gate_tile(g, 0, pair, left) * jnp.where(left, o_cmp[ha], o_cmp[hb])
            + _gate_tile(g, 1, pair, left) * jnp.where(left, o_slc[ha], o_slc[hb])
            + _gate_tile(g, 2, pair, left) * jnp.where(left, o_win[ha], o_win[hb]))


def _select_expand(n_keys):
    lane = jnp.arange(LANES, dtype=jnp.int32)[:, None]
    key = jnp.arange(n_keys, dtype=jnp.int32)[None, :]
    return ((lane % 4 == 3) & (lane // 4 == key // SLC_BLOCK)).astype(BF16)


def _nsa_prompt(q, p, cmp_rows, kvb, cw, batch, seq, tq):
    m = q.shape[0]
    assert seq == CMP_STRIDE * N_CMP_SLOTS and seq % tq == 0 and tq == LANES
    nq = seq // tq
    pe, w1, w2, tabs = cw
    full = lambda a: pl.BlockSpec(a.shape, lambda b, i, nd=a.ndim: (0,) * nd)
    key_blk = jnp.arange(seq, dtype=jnp.int32)[None, :] // SLC_BLOCK
    expand = (jnp.arange(LANES, dtype=jnp.int32)[:, None] == key_blk).astype(BF16)
    stat = pltpu.VMEM((NSA_HEADS, tq, LANES), F32)
    return pl.pallas_call(
        functools.partial(_nsa_prompt_kernel, tq=tq, seq=seq),
        grid=(batch, nq),
        in_specs=[pl.BlockSpec((tq, MIX_W), lambda b, i: (b * nq + i, 0)),
                  pl.BlockSpec((tq, LANES), lambda b, i: (b * nq + i, C_NG // LANES)),
                  pl.BlockSpec((seq, KV_W), lambda b, i: (b, 0)),
                  pl.BlockSpec((seq, KV_W), lambda b, i: (b, 1)),
                  pl.BlockSpec((seq, 4 * KV_W), lambda b, i: (b, 0)),
                  full(pe), full(w1), full(w2), full(tabs[0]), full(tabs[1]), full(tabs[2]), full(expand)],
        out_specs=pl.BlockSpec((tq, MIX_W), lambda b, i: (b * nq + i, 0)),
        out_shape=jax.ShapeDtypeStruct((m, MIX_W), F32),
        scratch_shapes=[pltpu.VMEM((N_CMP_SLOTS, KV_W), BF16), pltpu.VMEM((N_CMP_SLOTS, KV_W), BF16),
                        pltpu.VMEM((LANES, tq), F32), pltpu.VMEM((seq // SLC_BLOCK, tq), F32),
                        pltpu.VMEM((NSA_KV_HEADS, tq, LANES), BF16), stat, stat, stat],
        compiler_params=_cparams(("parallel", "arbitrary")),
        name="nsa_prompt",
    )(q, p, cmp_rows, cmp_rows, kvb, pe, w1, w2, *tabs, expand)


def _nsa_sample_kernel(*refs, past, dec, n_pages, n_seq):
    n_pg = n_seq * n_pages
    cpages, spages = refs[1:1 + n_pg], refs[1 + n_pg:1 + 2 * n_pg]
    (wbuf_ref, q_ref, ng_ref, slc_ref, win_ref, pe_ref, w1_ref, w2_ref, a_ref, bm_ref, bp_ref, exp_ref,
     o_ref, wout_ref, kslc_s, vslc_s) = refs[1 + 2 * n_pg:]
    for sb in range(n_seq):
        rows = slice(sb * dec, (sb + 1) * dec)
        pgs = slice(sb * n_pages, (sb + 1) * n_pages)
        _nsa_sample_one(cpages[pgs], spages[pgs], wbuf_ref.at[sb], q_ref.at[rows], ng_ref.at[rows], slc_ref.at[rows],
                        win_ref.at[rows], pe_ref, w1_ref, w2_ref, a_ref, bm_ref, bp_ref, exp_ref, o_ref.at[rows],
                        wout_ref.at[sb], kslc_s.at[sb], vslc_s.at[sb], past=past, dec=dec)


def _nsa_sample_one(cpages, spages, wbuf_ref, q_ref, ng_ref, slc_ref, win_ref, pe_ref, w1_ref, w2_ref, a_ref, bm_ref,
                    bp_ref, exp_ref, o_ref, wout_ref, kslc_s, vslc_s, *, past, dec):
    n_pages = len(spages)
    for pg in range(n_pages):
        r0 = pg * PAGE_SIZE
        kslc_s[r0:r0 + PAGE_SIZE, :] = spages[pg][0, :, :KV_W].astype(BF16)
        vslc_s[r0:r0 + PAGE_SIZE, :] = spages[pg][0, :, KV_W:].astype(BF16)

    n_rows = NSA_HEADS * dec
    tail = LANES - dec
    new_slc = jnp.concatenate([slc_ref[...], jnp.zeros((tail, 2 * KV_W), F32)], axis=0).astype(BF16)
    kslc_s[past:past + LANES, :] = new_slc[:, :KV_W]
    vslc_s[past:past + LANES, :] = new_slc[:, KV_W:]
    k_cmp, v_cmp = _compress(_paged_rows(cpages, 0), _paged_rows(cpages, 1), pe_ref, w1_ref, w2_ref,
                             a_ref, bm_ref, bp_ref)

    lane8 = lax.broadcasted_iota(jnp.int32, (dec, LANES), 1)
    lane = lax.broadcasted_iota(jnp.int32, (n_rows, LANES), 1)
    rowi = lax.broadcasted_iota(jnp.int32, (n_rows, LANES), 0)
    qpos = past + rowi % dec
    qs = jnp.concatenate([_group_query(q_ref, h, lane8) for h in range(NSA_HEADS)], axis=0).astype(BF16)

    cmask = (CMP_STRIDE * lane + CMP_BLOCK - 1) <= qpos
    p = _masked_softmax(_dot_nt(qs, k_cmp.astype(BF16)) * ATT_SCALE, cmask)
    o_cmp = _dot(p.astype(BF16), v_cmp.astype(BF16))

    sels = []
    qpos8 = past + lax.broadcasted_iota(jnp.int32, (dec, LANES), 0)
    for grp in range(NSA_KV_HEADS):
        base = grp * NSA_GROUP * dec
        psum = p[base:base + dec]
        for r in range(1, NSA_GROUP):
            psum = psum + p[base + r * dec:base + (r + 1) * dec]
        score, allowed = _block_scores(psum, qpos8, lane8)
        cnt = _rank_count(score, lane8, jnp.where(score < FORCE_SCORE, 1.0, 0.0))
        sel = jnp.where(allowed & (cnt < TOP_N), 1.0, 0.0)
        sels += [sel] * NSA_GROUP
    sel = jnp.concatenate(sels, axis=0).astype(BF16)
    n_keys = past + LANES
    kpos = lax.broadcasted_iota(jnp.int32, (n_rows, n_keys), 1)
    qpos_k = past + lax.broadcasted_iota(jnp.int32, (n_rows, n_keys), 0) % dec
    kmask = ((_dot(sel, exp_ref[...]) > 0.5) | (kpos >= past)) & (kpos <= qpos_k)
    p = _masked_softmax(_dot_nt(qs, kslc_s[...]) * ATT_SCALE, kmask)
    o_slc = _dot(p.astype(BF16), vslc_s[...])

    wb = wbuf_ref.shape[0]
    buf = wbuf_ref[...]
    new = win_ref[...]
    zpad = jnp.zeros((tail, KV_W), F32)
    k_win = jnp.concatenate([buf[:, :KV_W], new[:, :KV_W], zpad], axis=0).astype(BF16)
    v_win = jnp.concatenate([buf[:, KV_W:], new[:, KV_W:], zpad], axis=0).astype(BF16)
    n_w = wb + LANES
    kidx = lax.broadcasted_iota(jnp.int32, (n_rows, n_w), 1)
    kpos_w = past - wb + kidx
    d = past + lax.broadcasted_iota(jnp.int32, (n_rows, n_w), 0) % dec - kpos_w
    wmask = (d >= 0) & (d <= WINDOW) & (kpos_w >= 0)
    p = _masked_softmax(_dot_nt(qs, k_win) * ATT_SCALE, wmask)
    o_win = _dot(p.astype(BF16), v_win)
    wout_ref[:wb - dec, :] = buf[dec:, :]
    wout_ref[wb - dec:, :] = new

    g = _sigmoid(ng_ref[...])
    left = lane8 < NSA_HD
    for pair in range(NSA_HEADS // 2):
        tile = jnp.zeros((dec, LANES), F32)
        for br, o_b in enumerate((o_cmp, o_slc, o_win)):
            pa = _place(o_b[(2 * pair) * dec:(2 * pair + 1) * dec], 2 * pair, lane8)
            pb = _place(o_b[(2 * pair + 1) * dec:(2 * pair + 2) * dec], 2 * pair + 1, lane8)
            tile = tile + _gate_tile(g, br, pair, left) * jnp.where(left, pa, pb)
        o_ref[:, pair * LANES:(pair + 1) * LANES] = tile


def _nsa_sample(q, p, slc_new, win_new, pool_cmp, pool_slc, win_buf, page_table, layer, cw, batch, dec, n_seq):
    m = q.shape[0]
    n_pages = page_table.shape[1]
    past = n_pages * PAGE_SIZE
    n_pool = pool_cmp.shape[0] // 4
    wb = win_buf.shape[1]
    assert past == CMP_STRIDE * N_CMP_SLOTS and past % SLC_BLOCK == 0 and dec % 8 == 0 and dec <= SLC_BLOCK
    assert (past + dec - CMP_BLOCK) // CMP_STRIDE + 1 == N_CMP_SLOTS - 1 and wb % 8 == 0
    pe, w1, w2, tabs = cw
    full = lambda a: pl.BlockSpec(a.shape, lambda b, pt, nd=a.ndim: (0,) * nd)
    expand = _select_expand(past + LANES)
    assert batch % n_seq == 0
    n_pg = n_seq * n_pages
    page_id = lambda b, pt, j: layer * n_pool + pt[b * n_pg + j]
    cmp_pages = [pl.BlockSpec((1, CMP_STRIDE, PAGE_SIZE // CMP_STRIDE, 2 * KV_W),
                              lambda b, pt, j=j: (page_id(b, pt, j), 0, 0, 0)) for j in range(n_pg)]
    slc_pages = [pl.BlockSpec((1, PAGE_SIZE, 2 * KV_W), lambda b, pt, j=j: (page_id(b, pt, j), 0, 0))
                 for j in range(n_pg)]
    rows = lambda w, c=0: pl.BlockSpec((n_seq * dec, w), lambda b, pt, c=c: (b, c))
    grid_spec = pltpu.PrefetchScalarGridSpec(
        num_scalar_prefetch=1,
        grid=(batch // n_seq,),
        in_specs=cmp_pages + slc_pages
        + [pl.BlockSpec((n_seq, wb, 2 * KV_W), lambda b, pt: (layer * (batch // n_seq) + b, 0, 0)),
           rows(MIX_W), rows(LANES, C_NG // LANES), rows(2 * KV_W), rows(2 * KV_W),
           full(pe), full(w1), full(w2), full(tabs[0]), full(tabs[1]), full(tabs[2]), full(expand)],
        out_specs=[rows(MIX_W), pl.BlockSpec((n_seq, wb, 2 * KV_W), lambda b, pt: (b, 0, 0))],
        scratch_shapes=[pltpu.VMEM((n_seq, past + LANES, KV_W), BF16), pltpu.VMEM((n_seq, past + LANES, KV_W), BF16)],
    )
    return pl.pallas_call(
        functools.partial(_nsa_sample_kernel, past=past, dec=dec, n_pages=n_pages, n_seq=n_seq),
        grid_spec=grid_spec,
        out_shape=[jax.ShapeDtypeStruct((m, MIX_W), F32), jax.ShapeDtypeStruct((batch, wb, 2 * KV_W), F32)],
        compiler_params=_cparams(("parallel",)),
        name="nsa_sample",
    )(page_table.reshape(-1), *([pool_cmp] * n_pg), *([pool_slc] * n_pg), win_buf, q, p, slc_new, win_new,
      pe, w1, w2, *tabs, expand)


def _merge_kernel(x_ref, hg_ref, nsa_ref, ga_ref, gb_ref, wa_ref, wb_ref, wo_ref, o_ref):
    a = _dot(hg_ref[...].astype(BF16), wa_ref[...])
    b = _dot(nsa_ref[...].astype(BF16), wb_ref[...])
    merged = _sigmoid(ga_ref[...]) * a + _sigmoid(gb_ref[...]) * b
    o_ref[...] = x_ref[...] + _dot(merged.astype(BF16), wo_ref[...])


def _merge(x, o_hg, o_nsa, p, w_a, w_b, w_o, tm):
    m = x.shape[0]
    full = lambda a: pl.BlockSpec(a.shape, lambda i: (0, 0))
    return pl.pallas_call(
        _merge_kernel,
        grid=(m // tm,),
        in_specs=[pl.BlockSpec((tm, D_MODEL), lambda i: (i, 0)),
                  pl.BlockSpec((tm, MIX_W), lambda i: (i, 0)),
                  pl.BlockSpec((tm, MIX_W), lambda i: (i, 0)),
                  pl.BlockSpec((tm, D_MODEL), lambda i: (i, C_MG // D_MODEL)),
                  pl.BlockSpec((tm, D_MODEL), lambda i: (i, C_MG // D_MODEL + 1)),
                  full(w_a), full(w_b), full(w_o)],
        out_specs=pl.BlockSpec((tm, D_MODEL), lambda i: (i, 0)),
        out_shape=jax.ShapeDtypeStruct((m, D_MODEL), F32),
        compiler_params=_cparams(("parallel",)),
        name="merge",
    )(x, o_hg, o_nsa, p, p, w_a, w_b, w_o)


def _ffn_kernel(*refs, tm, seq, has_state):
    if has_state:
        x_ref, gn_ref, wg_ref, wu_ref, cw_ref, cb_ref, wd_ref, p1_ref, p2_ref, o_ref, gt_ref, xn_ref, acc_ref = refs
    else:
        x_ref, gn_ref, wg_ref, wu_ref, cw_ref, cb_ref, wd_ref, o_ref, gt_ref, xn_ref, acc_ref, carry_ref = refs
    i, f = pl.program_id(0), pl.program_id(1)

    @pl.when(f == 0)
    def _():
        x = x_ref[...]
        ms = jnp.mean(x * x, axis=-1, keepdims=True)
        xn_ref[...] = (x * lax.rsqrt(ms + NORM_EPS) * gn_ref[...]).astype(BF16)
        acc_ref[...] = jnp.zeros_like(acc_ref)

    xn = xn_ref[...]
    g = _dot(xn, wg_ref[...])
    u = _dot(xn, wu_ref[...])
    row = lax.broadcasted_iota(jnp.int32, g.shape, 0)
    g1 = pltpu.roll(g, 1, 0)
    g2 = pltpu.roll(g, 2, 0)
    if has_state:
        tpos = row % seq
        g1 = jnp.where(tpos == 0, p1_ref[...], g1)
        g2 = jnp.where(tpos < 2, p2_ref[...], g2)
        gt_ref[...] = g
    else:
        first = (i % (seq // tm)) == 0
        prev = jnp.where(first, 0.0, carry_ref[f])
        g1 = jnp.where(row == 0, prev[7:8, :], g1)
        g2 = jnp.where(row == 0, prev[6:7, :], jnp.where(row == 1, prev[7:8, :], g2))
        carry_ref[f] = g[tm - 8:, :]
        gt_ref[0] = g[tm - 8:, :]
    conv = cw_ref[0:1, :] * g2 + cw_ref[1:2, :] * g1 + cw_ref[2:3, :] * g + cb_ref[...]
    acc_ref[...] += _dot((_gelu_tanh(conv) * u).astype(BF16), wd_ref[...])

    @pl.when(f == pl.num_programs(1) - 1)
    def _():
        o_ref[...] = x_ref[...] + acc_ref[...]


def _ffn(x, gn, w_gu, conv_w, conv_b, w_d, prev, seq, tm, tf):
    m = x.shape[0]
    nf = D_FF // tf
    has_state = prev is not None
    in_specs = [pl.BlockSpec((tm, D_MODEL), lambda i, f: (i, 0)),
                pl.BlockSpec((1, D_MODEL), lambda i, f: (0, 0)),
                pl.BlockSpec((D_MODEL, tf), lambda i, f: (0, f)),
                pl.BlockSpec((D_MODEL, tf), lambda i, f: (0, nf + f)),
                pl.BlockSpec((CONV_W, tf), lambda i, f: (0, f)),
                pl.BlockSpec((1, tf), lambda i, f: (0, f)),
                pl.BlockSpec((tf, D_MODEL), lambda i, f: (f, 0))]
    args = [x, gn, w_gu, w_gu, conv_w, conv_b, w_d]
    scratch = [pltpu.VMEM((tm, D_MODEL), BF16), pltpu.VMEM((tm, D_MODEL), F32)]
    if has_state:
        assert tm % seq == 0
        in_specs += [pl.BlockSpec((tm, tf), lambda i, f: (i, f))] * 2
        args += list(prev)
        gt_spec = pl.BlockSpec((tm, tf), lambda i, f: (i, f))
        gt_shape = jax.ShapeDtypeStruct((m, D_FF), F32)
    else:
        assert seq % tm == 0
        gt_spec = pl.BlockSpec((1, 8, tf), lambda i, f: (i, 0, f))
        gt_shape = jax.ShapeDtypeStruct((m // tm, 8, D_FF), F32)
        scratch.append(pltpu.VMEM((nf, 8, tf), F32))
    return pl.pallas_call(
        functools.partial(_ffn_kernel, tm=tm, seq=seq, has_state=has_state),
        grid=(m // tm, nf),
        in_specs=in_specs,
        out_specs=[pl.BlockSpec((tm, D_MODEL), lambda i, f: (i, 0)), gt_spec],
        out_shape=[jax.ShapeDtypeStruct((m, D_MODEL), F32), gt_shape],
        scratch_shapes=scratch,
        compiler_params=_cparams(("arbitrary", "arbitrary")),
        name="ffn",
    )(*args)


def _prepare_weights(w_in, hg_lower, cmp_pe, cmp_w1, cmp_w2, w_branch, w_out, w_gate_up, w_down):
    depth = w_in.shape[0]
    c_rest = 4 * MIX_W
    c_ng = c_rest + MIX_W + 6 * KV_W
    n_gate = 3 * NSA_HEADS
    w_in_p = jnp.concatenate([
        w_in[:, :, :c_rest], w_in[:, :, c_ng + n_gate:], w_in[:, :, c_rest:c_ng], w_in[:, :, c_ng:c_ng + n_gate],
        jnp.zeros((depth, D_MODEL, N_IN - w_in.shape[2]), w_in.dtype)], axis=2).astype(BF16)
    lbs = jax.nn.softmax(hg_lower.astype(F32), axis=0)
    lbs = (jnp.cumsum(lbs, axis=0) - lbs[0:1]).reshape(depth, HG_HEADS, 1, HG_DK)
    def block_diag(w):
        w = w.astype(BF16)
        z = jnp.zeros_like(w)
        return jnp.concatenate([jnp.concatenate([w, z], axis=-1), jnp.concatenate([z, w], axis=-1)], axis=-2)

    pe = jnp.concatenate([cmp_pe] * NSA_KV_HEADS, axis=-1)
    return (w_in_p, lbs, pe, block_diag(cmp_w1), block_diag(cmp_w2), w_branch.astype(BF16), w_out.astype(BF16),
            w_gate_up.astype(BF16), w_down.astype(BF16))


def _trunk(x, pos, caches, page_table, weights, batch, seq, cfg):
    (norm_mix, w_in_p, lbs, hg_norm, pe, w1, w2, w_br, w_out, norm_ffn, w_gu, conv_w, conv_b, w_down, norm_final) = weights
    depth = w_in_p.shape[0]
    tm = cfg["tm"]
    tables = _rope_tables(pos)
    if seq < tm:
        tables = tuple(jnp.tile(t, (tm // seq, 1)) for t in tables)
    cmp_tabs = _rope_tables(jnp.arange(N_CMP_SLOTS, dtype=jnp.int32) * CMP_STRIDE + CMP_BLOCK - 1)
    outs = [[] for _ in range(5)]
    for l in range(depth):
        p = _norm_matmul(x, norm_mix[l][None], w_in_p[l], cfg["nm_tm"], cfg["tn"])
        q_rot, cmp_rows, slc_rows, win_rows, kvb = _prep(p, tables, tm)
        s0 = None if caches is None else caches[3][l]
        o_hg, s_hg = _gla(p, lbs[l], hg_norm[l][None], s0, batch, seq, cfg["gla_rows"], cfg["chunk"], cfg["sub"],
                          cfg["gla_mm"], cfg["gla_heads"])
        cw = (pe[l], w1[l], w2[l], cmp_tabs)
        if caches is None:
            o_nsa = _nsa_prompt(q_rot, p, cmp_rows, kvb, cw, batch, seq, LANES)
            new_win = win_rows.reshape(batch, seq, 2 * KV_W)[:, -min(WINDOW, seq):]
        else:
            o_nsa, new_win = _nsa_sample(q_rot, p, slc_rows, win_rows, caches[0], caches[1], caches[2], page_table, l,
                                         cw, batch, seq, cfg["nsa_seqs"])
        x = _merge(x, o_hg, o_nsa, p, w_br[l, 0], w_br[l, 1], w_out[l], tm)
        if caches is None:
            x, g_tail = _ffn(x, norm_ffn[l][None], w_gu[l], conv_w[l], conv_b[l][None], w_down[l], None, seq,
                             cfg["ffn_tm"], cfg["tf"])
            per = seq // cfg["ffn_tm"]
            c_st = g_tail[per - 1::per, 8 - (CONV_W - 1):, :]
        else:
            st = caches[4][l]
            prev2 = jnp.pad(st, ((0, 0), (0, seq - (CONV_W - 1)), (0, 0))).reshape(batch * seq, D_FF)
            prev1 = jnp.pad(st[:, 1:], ((0, 0), (0, seq - 1), (0, 0))).reshape(batch * seq, D_FF)
            x, g_all = _ffn(x, norm_ffn[l][None], w_gu[l], conv_w[l], conv_b[l][None], w_down[l], (prev1, prev2), seq,
                            cfg["ffn_tm"], cfg["tf"])
            c_st = g_all.reshape(batch, seq, D_FF)[:, seq - (CONV_W - 1):]
        kv6 = lambda a: a.reshape(batch, -1, 2, NSA_KV_HEADS, NSA_HD)
        outs[0].append(kv6(cmp_rows))
        outs[1].append(kv6(slc_rows))
        outs[2].append(kv6(new_win))
        outs[3].append(s_hg)
        outs[4].append(c_st)
    y = _rmsnorm(x, norm_final[None], tm).reshape(batch, seq, D_MODEL)
    return (y,) + tuple(jnp.stack(o) for o in outs)


def kernel(x_prompt, x_sample, cache_cmp_kv, cache_slc_kv, cache_win_kv, state_hgrn, state_conv, page_table, norm_mix, w_in, hg_lower, hg_norm, cmp_pe, cmp_w1, cmp_w2, w_branch, w_out, norm_ffn, w_gate_up, conv_w, conv_b, w_down, norm_final):
    (w_in_p, lbs, pe, w1, w2, w_br, w_o, w_gu, w_d) = _prepare_weights(
        w_in, hg_lower, cmp_pe, cmp_w1, cmp_w2, w_branch, w_out, w_gate_up, w_down)
    weights = (norm_mix, w_in_p, lbs, hg_norm, pe, w1, w2, w_br, w_o, norm_ffn, w_gu, conv_w, conv_b, w_d, norm_final)
    b_p, l_p, _ = x_prompt.shape
    b_s, l_s, _ = x_sample.shape
    depth, n_pool = cache_cmp_kv.shape[:2]
    past = page_table.shape[1] * PAGE_SIZE
    pos_p = jnp.arange(l_p, dtype=jnp.int32)
    pos_s = past + jnp.arange(l_s, dtype=jnp.int32)
    cfg_p = dict(tm=1024, nm_tm=1024, tn=1408, gla_rows=256, chunk=64, sub=16, gla_mm=BF16, gla_heads=2, ffn_tm=512,
                 tf=1408)
    cfg_s = dict(tm=256, nm_tm=1024, tn=1408, gla_rows=l_s, chunk=l_s, sub=l_s, gla_mm=F32, gla_heads=HG_HEADS,
                 nsa_seqs=2, ffn_tm=256, tf=1408)
    res_p = _trunk(x_prompt.reshape(b_p * l_p, D_MODEL), pos_p, None, None, weights, b_p, l_p, cfg_p)
    caches = (cache_cmp_kv.reshape(depth * n_pool, PAGE_SIZE // CMP_STRIDE, CMP_STRIDE, 2 * KV_W).transpose(0, 2, 1, 3),
              cache_slc_kv.reshape(depth * n_pool, PAGE_SIZE, 2 * KV_W),
              cache_win_kv.reshape(depth * b_s, -1, 2 * KV_W), state_hgrn, state_conv)
    res_s = _trunk(x_sample.reshape(b_s * l_s, D_MODEL), pos_s, caches, page_table, weights, b_s, l_s, cfg_s)
    out = []
    for a, b in zip(res_p, res_s):
        out += [a, b]
    out[7] = out[7].reshape(depth, b_s, -1, 2, NSA_KV_HEADS, NSA_HD)
    return tuple(out)
```

```python
import functools

import jax
import jax.numpy as jnp
from jax import lax
from jax.experimental import pallas as pl
from jax.experimental.pallas import tpu as pltpu

F32 = jnp.float32
BF16 = jnp.bfloat16

D_MODEL = 1024
MIX_W = 512
HG_DK = 128
HG_HEADS = 4
HG_DV = 128
NSA_HD = 64
NSA_HEADS = 8
NSA_KV_HEADS = 2
NSA_GROUP = 4
KV_W = NSA_KV_HEADS * NSA_HD
CMP_BLOCK = 32
CMP_STRIDE = 16
SLC_BLOCK = 64
TOP_N = 16
WINDOW = 512
ROPE_THETA = 500000.0
ROPE_DIM = 16
D_FF = 2816
CONV_W = 3
PAGE_SIZE = 128
NORM_EPS = 1e-6
NEG_INF = -1e30
LB_FLOOR = 1e-30
FORCE_SCORE = 1e9
ATT_SCALE = NSA_HD ** -0.5
LOG2E = 1.4426950408889634

LANES = 128
N_CMP_SLOTS = 128
SLC_CHUNK = 512
CMP_PITCH = 24
VMEM_LIMIT = 56 * 1024 * 1024

C_HQ, C_HF, C_HI, C_HG = 0, 512, 1024, 1536
C_MG = 2048
C_NQ = 4096
C_KV = 4608
C_NG = 5376
N_IN = 5632


def _cparams(sem):
    return pltpu.CompilerParams(dimension_semantics=sem, vmem_limit_bytes=VMEM_LIMIT)


def _sigmoid(x):
    return 1.0 / (1.0 + jnp.exp(-x))


def _gelu_tanh(x):
    return 0.5 * x * (1.0 + jnp.tanh(0.7978845608028654 * (x + 0.044715 * (x * x * x))))


def _dot(a, b):
    return jnp.dot(a, b, preferred_element_type=F32)


def _dot_nt(a, b):
    return lax.dot_general(a, b, (((1,), (1,)), ((), ())), preferred_element_type=F32)


def _masked_softmax(s, mask):
    s = jnp.where(mask, s, NEG_INF)
    m = jnp.max(s, axis=-1, keepdims=True)
    e = jnp.where(mask, jnp.exp(s - m), 0.0)
    return e * (1.0 / jnp.maximum(jnp.sum(e, axis=-1, keepdims=True), 1e-30))


def _norm_matmul_kernel(x_ref, g_ref, w_ref, o_ref, xn_ref):
    @pl.when(pl.program_id(1) == 0)
    def _():
        x = x_ref[...]
        ms = jnp.mean(x * x, axis=-1, keepdims=True)
        xn_ref[...] = (x * lax.rsqrt(ms + NORM_EPS) * g_ref[...]).astype(BF16)

    o_ref[...] = _dot(xn_ref[...], w_ref[...])


def _norm_matmul(x, g, w, tm, tn):
    m, k = x.shape
    n = w.shape[1]
    return pl.pallas_call(
        _norm_matmul_kernel,
        grid=(m // tm, n // tn),
        in_specs=[pl.BlockSpec((tm, k), lambda i, j: (i, 0)),
                  pl.BlockSpec((1, k), lambda i, j: (0, 0)),
                  pl.BlockSpec((k, tn), lambda i, j: (0, j))],
        out_specs=pl.BlockSpec((tm, tn), lambda i, j: (i, j)),
        out_shape=jax.ShapeDtypeStruct((m, n), F32),
        scratch_shapes=[pltpu.VMEM((tm, k), BF16)],
        compiler_params=_cparams(("parallel", "arbitrary")),
        name="norm_matmul",
    )(x, g, w)


def _rmsnorm_kernel(x_ref, g_ref, o_ref):
    x = x_ref[...]
    ms = jnp.mean(x * x, axis=-1, keepdims=True)
    o_ref[...] = x * lax.rsqrt(ms + NORM_EPS) * g_ref[...]


def _rmsnorm(x, g, tm):
    m, k = x.shape
    return pl.pallas_call(
        _rmsnorm_kernel,
        grid=(m // tm,),
        in_specs=[pl.BlockSpec((tm, k), lambda i: (i, 0)), pl.BlockSpec((1, k), lambda i: (0, 0))],
        out_specs=pl.BlockSpec((tm, k), lambda i: (i, 0)),
        out_shape=jax.ShapeDtypeStruct((m, k), F32),
        compiler_params=_cparams(("parallel",)),
        name="final_norm",
    )(x, g)


def _rope_tables(pos):
    half = ROPE_DIM // 2
    inv = jnp.float32(ROPE_THETA) ** (-jnp.arange(half, dtype=F32) * 2.0 / ROPE_DIM)
    ang = pos.astype(F32)[:, None] * inv[None, :]
    cos, sin = jnp.cos(ang), jnp.sin(ang)
    t = pos.shape[0]
    z8 = jnp.zeros((t, half), F32)
    rest = NSA_HD - ROPE_DIM
    a = jnp.concatenate([cos, cos, jnp.ones((t, rest), F32)], axis=1)
    bm = jnp.concatenate([-sin, z8, jnp.zeros((t, rest), F32)], axis=1)
    bp = jnp.concatenate([z8, sin, jnp.zeros((t, rest), F32)], axis=1)
    return tuple(jnp.concatenate([v, v], axis=1) for v in (a, bm, bp))


def _rope(x, a, bm, bp):
    half = ROPE_DIM // 2
    return x * a + pltpu.roll(x, LANES - half, 1) * bm + pltpu.roll(x, half, 1) * bp


def _prep_kernel(nq_ref, kc_ref, vc_ref, ks_ref, vs_ref, kw_ref, vw_ref, a_ref, bm_ref, bp_ref,
                 q_ref, cmp_ref, slc_ref, win_ref, kvb_ref):
    a, bm, bp = a_ref[...], bm_ref[...], bp_ref[...]
    for c in range(MIX_W // LANES):
        q_ref[:, c * LANES:(c + 1) * LANES] = _rope(nq_ref[:, c * LANES:(c + 1) * LANES], a, bm, bp)
    cmp_ref[:, :KV_W] = kc_ref[...]
    cmp_ref[:, KV_W:] = vc_ref[...]
    ks = _rope(ks_ref[...], a, bm, bp)
    vs = vs_ref[...]
    slc_ref[:, :KV_W] = ks
    slc_ref[:, KV_W:] = vs
    kw = _rope(kw_ref[...], a, bm, bp)
    vw = vw_ref[...]
    win_ref[:, :KV_W] = kw
    win_ref[:, KV_W:] = vw
    kvb_ref[:, 0 * KV_W:1 * KV_W] = ks.astype(BF16)
    kvb_ref[:, 1 * KV_W:2 * KV_W] = vs.astype(BF16)
    kvb_ref[:, 2 * KV_W:3 * KV_W] = kw.astype(BF16)
    kvb_ref[:, 3 * KV_W:4 * KV_W] = vw.astype(BF16)


def _prep(p, tables, tm):
    m = p.shape[0]
    nt = tables[0].shape[0] // tm
    kvb0 = C_KV // KV_W
    col = lambda c: pl.BlockSpec((tm, KV_W), lambda i, c=c: (i, kvb0 + c))
    tab = pl.BlockSpec((tm, LANES), lambda i: (i % nt, 0))
    row = lambda w: pl.BlockSpec((tm, w), lambda i: (i, 0))
    return pl.pallas_call(
        _prep_kernel,
        grid=(m // tm,),
        in_specs=[pl.BlockSpec((tm, MIX_W), lambda i: (i, C_NQ // MIX_W))] + [col(c) for c in range(6)] + [tab] * 3,
        out_specs=[row(MIX_W), row(2 * KV_W), row(2 * KV_W), row(2 * KV_W), row(4 * KV_W)],
        out_shape=[jax.ShapeDtypeStruct((m, MIX_W), F32)] + [jax.ShapeDtypeStruct((m, 2 * KV_W), F32)] * 3
        + [jax.ShapeDtypeStruct((m, 4 * KV_W), BF16)],
        compiler_params=_cparams(("parallel",)),
        name="prep",
    )(p, p, p, p, p, p, p, *tables)


def _gla_kernel(*refs, chunk, sub, n_chunks, has_s0, mm_dtype, heads):
    if has_s0:
        q_ref, f_ref, i_ref, g_ref, lb_ref, ng_ref, s0_ref, o_ref, so_ref, st_ref, b_s, k_s = refs
    else:
        q_ref, f_ref, i_ref, g_ref, lb_ref, ng_ref, o_ref, so_ref, st_ref, b_s, k_s = refs
        s0_ref = None
    t = pl.program_id(2)

    @pl.when(t == 0)
    def _():
        for hh in range(heads):
            st_ref[hh] = jnp.zeros((HG_DV, HG_DK), F32) if s0_ref is None else s0_ref[0, hh].T

    for hh in range(heads):
        cols = slice(hh * HG_DK, (hh + 1) * HG_DK)
        _gla_head(q_ref.at[:, cols], f_ref.at[:, cols], i_ref.at[:, cols], g_ref.at[:, cols], lb_ref[hh], ng_ref[...],
                  o_ref.at[:, cols], st_ref.at[hh], b_s.at[hh], k_s.at[hh],
                  chunk=chunk, sub=sub, n_chunks=n_chunks, mm_dtype=mm_dtype)

    @pl.when(t == pl.num_programs(2) - 1)
    def _():
        for hh in range(heads):
            so_ref[0, hh] = st_ref[hh].T


def _gla_head(q_ref, f_ref, i_ref, g_ref, lb, ng, o_ref, st_ref, b_all, k_all, *, chunk, sub, n_chunks, mm_dtype):
    log_lb = jnp.log(jnp.maximum(lb, LB_FLOOR))
    log_1m = jnp.log1p(-lb)
    one_m = 1.0 - lb
    row = lax.broadcasted_iota(jnp.int32, (chunk, HG_DK), 0)
    row_s = lax.broadcasted_iota(jnp.int32, (sub, HG_DK), 0)
    lane_s = lax.broadcasted_iota(jnp.int32, (sub, LANES), 1)
    pad = LANES - chunk

    st = st_ref[...]
    for c in range(n_chunks):
        r0 = c * chunk
        b_s, k_s = b_all.at[c], k_all.at[c]
        fr = f_ref[pl.ds(r0, chunk), :]
        qr = q_ref[pl.ds(r0, chunk), :]
        v = i_ref[pl.ds(r0, chunk), :]
        log_sig = jnp.minimum(fr, 0.0) - jnp.log1p(jnp.exp(-jnp.abs(fr)))
        bb = log_1m + log_sig
        log_f = jnp.maximum(log_lb, bb) + jnp.log1p(jnp.exp(-jnp.abs(log_lb - bb)))
        k = one_m / (1.0 + jnp.exp(fr))
        q = qr * _sigmoid(qr) * (HG_DK ** -0.5)
        b = log_f
        sh = 1
        while sh < chunk:
            b = b + jnp.where(row >= sh, pltpu.roll(b, sh, 0), 0.0)
            sh *= 2
        b_s[...] = b
        k_s[...] = k
        o = _dot_nt((q * jnp.exp(b)).astype(mm_dtype), st.astype(mm_dtype))
        pad_rows = lambda a: jnp.concatenate([a, jnp.zeros((pad, HG_DK), F32)], axis=0) if pad > 0 else a
        v_p = pad_rows(v)
        parts = []
        for blk in range(chunk // sub):
            lo = blk * sub
            b_i = b[lo:lo + sub]
            q_i = q[lo:lo + sub]
            if blk > 0:
                ref_b = b_s[lo - 1:lo, :]
                q_d = q_i * jnp.exp(b_i - ref_b)
                k_d = jnp.where(row < lo, k * jnp.exp(jnp.minimum(ref_b - b, 0.0)), 0.0)
                att = _dot_nt(q_d.astype(mm_dtype), pad_rows(k_d).astype(mm_dtype))
            else:
                att = jnp.zeros((sub, LANES), F32)
            zs = []
            for s in range(sub):
                e = jnp.exp(jnp.where(row_s >= s, b_i - b_s[lo + s:lo + s + 1, :], NEG_INF))
                zs.append((q_i * e * k_s[lo + s:lo + s + 1, :]).astype(mm_dtype))
            red = _dot(jnp.concatenate(zs, axis=0), jnp.ones((HG_DK, LANES), mm_dtype))
            for s in range(sub):
                att = jnp.where(lane_s == lo + s, red[s * sub:(s + 1) * sub], att)
            parts.append(att)
        att = parts[0] if len(parts) == 1 else jnp.concatenate(parts, axis=0)
        o = o + _dot(att.astype(mm_dtype), v_p.astype(mm_dtype))
        b_last = b_s[chunk - 1:chunk, :]
        k_dec = pad_rows(k * jnp.exp(b_last - b))
        st = jnp.exp(b_last) * st + _dot(v_p.T.astype(mm_dtype), k_dec.astype(mm_dtype))
        gr = g_ref[pl.ds(r0, chunk), :]
        ms = jnp.mean(o * o, axis=-1, keepdims=True)
        o_ref[pl.ds(r0, chunk), :] = o * lax.rsqrt(ms + NORM_EPS) * ng * (gr * _sigmoid(gr))
    st_ref[...] = st


def _gla(p, lb, ng, s0, batch, seq, rows, chunk, sub, mm_dtype, heads):
    m = p.shape[0]
    nt = seq // rows
    has_s0 = s0 is not None
    w = heads * HG_DK
    col = lambda c0: pl.BlockSpec((rows, w), lambda b, h, t, c0=c0: (b * nt + t, c0 // w + h))
    in_specs = [col(C_HQ), col(C_HF), col(C_HI), col(C_HG),
                pl.BlockSpec((heads, 1, HG_DK), lambda b, h, t: (h, 0, 0)),
                pl.BlockSpec((1, HG_DV), lambda b, h, t: (0, 0))]
    args = [p, p, p, p, lb, ng]
    if has_s0:
        in_specs.append(pl.BlockSpec((1, heads, HG_DK, HG_DV), lambda b, h, t: (b, h, 0, 0)))
        args.append(s0)
    return pl.pallas_call(
        functools.partial(_gla_kernel, chunk=chunk, sub=sub, n_chunks=rows // chunk, has_s0=has_s0, mm_dtype=mm_dtype,
                          heads=heads),
        grid=(batch, HG_HEADS // heads, nt),
        in_specs=in_specs,
        out_specs=[pl.BlockSpec((rows, w), lambda b, h, t: (b * nt + t, h)),
                   pl.BlockSpec((1, heads, HG_DK, HG_DV), lambda b, h, t: (b, h, 0, 0))],
        out_shape=[jax.ShapeDtypeStruct((m, MIX_W), F32),
                   jax.ShapeDtypeStruct((batch, HG_HEADS, HG_DK, HG_DV), F32)],
        scratch_shapes=[pltpu.VMEM((heads, HG_DV, HG_DK), F32),
                        pltpu.VMEM((heads, rows // chunk, chunk, HG_DK), F32),
                        pltpu.VMEM((heads, rows // chunk, chunk, HG_DK), F32)],
        compiler_params=_cparams(("parallel", "parallel", "arbitrary")),
        name="gla",
    )(*args)


def _compress_one(row_of_chunks, kv, pe_ref, w1_ref, w2_ref):
    n = N_CMP_SLOTS
    acc_a = jnp.zeros((n, KV_W), F32)
    acc_b = jnp.zeros((n, KV_W), F32)
    for j in range(CMP_STRIDE):
        x = row_of_chunks(j)
        acc_a = acc_a + _dot((x + pe_ref[kv, j:j + 1, :]).astype(BF16), w1_ref[kv, j])
        acc_b = acc_b + _dot((x + pe_ref[kv, CMP_STRIDE + j:CMP_STRIDE + j + 1, :]).astype(BF16),
                             w1_ref[kv, CMP_STRIDE + j])
    h = _gelu_tanh(acc_a + pltpu.roll(acc_b, n - 1, 0))
    out = _dot(h.astype(BF16), w2_ref[kv])
    slot = lax.broadcasted_iota(jnp.int32, (n, KV_W), 0)
    return jnp.where(slot < n - 1, out, 0.0)


def _compress(k_rows, v_rows, pe_ref, w1_ref, w2_ref, a_ref, bm_ref, bp_ref):
    k_cmp = _rope(_compress_one(k_rows, 0, pe_ref, w1_ref, w2_ref), a_ref[...], bm_ref[...], bp_ref[...])
    return k_cmp, _compress_one(v_rows, 1, pe_ref, w1_ref, w2_ref)


def _strided_rows(rows_ref, pitch=CMP_STRIDE):
    return lambda j: rows_ref[pl.ds(j, N_CMP_SLOTS, stride=pitch), :]


def _group_query(q_ref, head, lane):
    grp, pair, half = head // NSA_GROUP, head // 2, head % 2
    x = q_ref[:, pair * LANES:(pair + 1) * LANES]
    if half != grp:
        x = pltpu.roll(x, NSA_HD, 1)
    return jnp.where((lane >= grp * NSA_HD) & (lane < (grp + 1) * NSA_HD), x, 0.0)


def _block_scores(psum, qpos, lane):
    w = psum
    for k in range(1, CMP_BLOCK // CMP_STRIDE + SLC_BLOCK // CMP_STRIDE - 1):
        w = w + jnp.where(lane >= k, pltpu.roll(psum, k, 1), 0.0)
    blk = lane // 4
    cur = qpos // SLC_BLOCK
    forced = (blk == 0) | (blk == cur) | (blk == cur - 1)
    allowed = (blk * SLC_BLOCK <= qpos) & (lane % 4 == 3)
    return jnp.where(allowed, jnp.where(forced, FORCE_SCORE, w), NEG_INF), allowed


def _rank_count(score, lane, cnt):
    for s in range(LANES // 4):
        c = 4 * s + 3
        col = score[:, c:c + 1]
        cnt = cnt + jnp.where((col > score) | ((col == score) & (lane > c)), 1.0, 0.0)
    return cnt


def _place(res, head, lane):
    grp, half = head // NSA_GROUP, head % 2
    return res if half == grp else pltpu.roll(res, NSA_HD, 1)


def _gate_tile(g, branch, pair, left):
    c = branch * NSA_HEADS + 2 * pair
    return jnp.where(left, g[:, c:c + 1], g[:, c + 1:c + 2])


def _nsa_prompt_kernel(q_ref, ng_ref, kcr_ref, vcr_ref, kvb_ref, pe_ref, w1_ref, w2_ref, a_ref, bm_ref, bp_ref, exp_ref,
                       o_ref, kc_s, vc_s, wt_s, sc_s, sel_s, m_s, l_s, acc_s, *, tq, seq):
    i = pl.program_id(1)

    @pl.when(i == 0)
    def _():
        k_cmp, v_cmp = _compress(_strided_rows(kcr_ref), _strided_rows(vcr_ref), pe_ref, w1_ref, w2_ref,
                                 a_ref, bm_ref, bp_ref)
        kc_s[...] = k_cmp.astype(BF16)
        vc_s[...] = v_cmp.astype(BF16)

    q0 = i * tq
    lane = lax.broadcasted_iota(jnp.int32, (tq, LANES), 1)
    left = lane < NSA_HD
    qpos = q0 + lax.broadcasted_iota(jnp.int32, (tq, LANES), 0)
    g = _sigmoid(ng_ref[...])
    qm = [(_group_query(q_ref, h, lane) * (ATT_SCALE * LOG2E)).astype(BF16) for h in range(NSA_HEADS)]
    qg = [jnp.concatenate(qm[grp * NSA_GROUP:(grp + 1) * NSA_GROUP], axis=0) for grp in range(NSA_KV_HEADS)]
    rows_of = lambda a, r: a[r * tq:(r + 1) * tq]
    kc, vc = kc_s[...], vc_s[...]
    cmask = (CMP_STRIDE * lane + CMP_BLOCK - 1) <= qpos
    n_blk = seq // SLC_BLOCK
    blk_t = lax.broadcasted_iota(jnp.int32, (n_blk, tq), 0)
    qpos_t = q0 + lax.broadcasted_iota(jnp.int32, (n_blk, tq), 1)
    cur_t = qpos_t // SLC_BLOCK
    forced_t = (blk_t == 0) | (blk_t == cur_t) | (blk_t == cur_t - 1)
    allowed_t = blk_t * SLC_BLOCK <= qpos_t

    o_cmp, o_slc, o_win = [None] * NSA_HEADS, [None] * NSA_HEADS, [None] * NSA_HEADS
    for grp in range(NSA_KV_HEADS):
        psum = jnp.zeros((tq, LANES), F32)
        s_all = _dot_nt(qg[grp], kc)
        ps = []
        for r in range(NSA_GROUP):
            s = jnp.where(cmask, rows_of(s_all, r), NEG_INF)
            e = jnp.where(cmask, jnp.exp2(s - jnp.max(s, axis=-1, keepdims=True)), 0.0)
            p = e * (1.0 / jnp.maximum(jnp.sum(e, axis=-1, keepdims=True), 1e-30))
            psum = psum + p
            ps.append(p.astype(BF16))
        o_all = _dot(jnp.concatenate(ps, axis=0), vc)
        for r in range(NSA_GROUP):
            o_cmp[grp * NSA_GROUP + r] = _place(rows_of(o_all, r), grp * NSA_GROUP + r, lane)
        w = psum
        for k in range(1, CMP_BLOCK // CMP_STRIDE + SLC_BLOCK // CMP_STRIDE - 1):
            w = w + jnp.where(lane >= k, pltpu.roll(psum, k, 1), 0.0)
        wt_s[...] = w.T
        imp_t = wt_s[pl.ds(SLC_BLOCK // CMP_STRIDE - 1, n_blk, stride=SLC_BLOCK // CMP_STRIDE), :]
        score = jnp.where(allowed_t, jnp.where(forced_t, FORCE_SCORE, imp_t), NEG_INF)
        sc_s[...] = score
        cnt = jnp.zeros((n_blk, tq), F32)
        for c in range(n_blk):
            other = sc_s[c:c + 1, :]
            cnt = cnt + jnp.where((other > score) | ((other == score) & (blk_t > c)), 1.0, 0.0)
        sel_t = jnp.where(allowed_t & (cnt < TOP_N), 1.0, 0.0)
        sel_s[grp] = jnp.concatenate([sel_t, jnp.zeros((LANES - n_blk, tq), F32)], axis=0).T.astype(BF16)

    for h in range(NSA_HEADS):
        m_s[h] = jnp.full((tq, LANES), NEG_INF, F32)
        l_s[h] = jnp.zeros((tq, LANES), F32)
        acc_s[h] = jnp.zeros((tq, LANES), F32)
    kc_w = SLC_CHUNK
    for ck in range(seq // kc_w):
        @pl.when(ck * kc_w <= q0)
        def _(ck=ck):
            kpos = ck * kc_w + lax.broadcasted_iota(jnp.int32, (tq, kc_w), 1)
            causal = kpos <= q0 + lax.broadcasted_iota(jnp.int32, (tq, kc_w), 0)
            k_c = kvb_ref[ck * kc_w:(ck + 1) * kc_w, 0 * KV_W:1 * KV_W]
            v_c = kvb_ref[ck * kc_w:(ck + 1) * kc_w, 1 * KV_W:2 * KV_W]
            for grp in range(NSA_KV_HEADS):
                picked = _dot(sel_s[grp], exp_ref[:, ck * kc_w:(ck + 1) * kc_w]) > 0.5
                bias = jnp.where(picked & causal, 0.0, NEG_INF)
                s_all = _dot_nt(qg[grp], k_c)
                ps, alphas = [], []
                for r in range(NSA_GROUP):
                    h = grp * NSA_GROUP + r
                    s = rows_of(s_all, r) + bias
                    m_prev = m_s[h]
                    m_new = jnp.maximum(m_prev, jnp.max(s, axis=-1, keepdims=True))
                    alpha = jnp.exp2(m_prev - m_new)
                    p = jnp.exp2(s - jnp.concatenate([m_new] * (kc_w // LANES), axis=1))
                    l_s[h] = alpha * l_s[h] + jnp.sum(p, axis=-1, keepdims=True)
                    m_s[h] = m_new
                    ps.append(p.astype(BF16))
                    alphas.append(alpha)
                pv = _dot(jnp.concatenate(ps, axis=0), v_c)
                for r in range(NSA_GROUP):
                    h = grp * NSA_GROUP + r
                    acc_s[h] = alphas[r] * acc_s[h] + rows_of(pv, r)

    n_win = WINDOW + tq
    w0 = pl.multiple_of(jnp.maximum(q0 - WINDOW, 0), tq)
    dwin = (q0 - w0) + lax.broadcasted_iota(jnp.int32, (tq, n_win), 0) - lax.broadcasted_iota(jnp.int32, (tq, n_win), 1)
    wbias = jnp.where((dwin >= 0) & (dwin <= WINDOW), 0.0, NEG_INF)
    k_win = kvb_ref[pl.ds(w0, n_win), 2 * KV_W:3 * KV_W]
    v_win = kvb_ref[pl.ds(w0, n_win), 3 * KV_W:4 * KV_W]
    for grp in range(NSA_KV_HEADS):
        s_all = _dot_nt(qg[grp], k_win)
        ps, inv = [], []
        for r in range(NSA_GROUP):
            s = rows_of(s_all, r) + wbias
            p = jnp.exp2(s - jnp.max(s, axis=-1, keepdims=True))
            inv.append(1.0 / jnp.sum(p, axis=-1, keepdims=True))
            ps.append(p.astype(BF16))
        o_all = _dot(jnp.concatenate(ps, axis=0), v_win)
        for r in range(NSA_GROUP):
            h = grp * NSA_GROUP + r
            o_win[h] = _place(rows_of(o_all, r) * inv[r], h, lane)
            o_slc[h] = _place(acc_s[h] * (1.0 / l_s[h]), h, lane)
    for pair in range(NSA_HEADS // 2):
        ha, hb = 2 * pair, 2 * pair + 1
        o_ref[:, pair * LANES:(pair + 1) * LANES] = (
            _gate_tile(g, 0, pair, left) * jnp.where(left, o_cmp[ha], o_cmp[hb])
            + _gate_tile(g, 1, pair, left) * jnp.where(left, o_slc[ha], o_slc[hb])
            + _gate_tile(g, 2, pair, left) * jnp.where(left, o_win[ha], o_win[hb]))


def _select_expand(n_keys):
    lane = jnp.arange(LANES, dtype=jnp.int32)[:, None]
    key = jnp.arange(n_keys, dtype=jnp.int32)[None, :]
    return ((lane % 4 == 3) & (lane // 4 == key // SLC_BLOCK)).astype(BF16)


def _nsa_prompt(q, p, cmp_rows, kvb, cw, batch, seq, tq):
    m = q.shape[0]
    assert seq == CMP_STRIDE * N_CMP_SLOTS and seq % tq == 0 and tq == LANES
    nq = seq // tq
    pe, w1, w2, tabs = cw
    full = lambda a: pl.BlockSpec(a.shape, lambda b, i, nd=a.ndim: (0,) * nd)
    key_blk = jnp.arange(seq, dtype=jnp.int32)[None, :] // SLC_BLOCK
    expand = (jnp.arange(LANES, dtype=jnp.int32)[:, None] == key_blk).astype(BF16)
    stat = pltpu.VMEM((NSA_HEADS, tq, LANES), F32)
    return pl.pallas_call(
        functools.partial(_nsa_prompt_kernel, tq=tq, seq=seq),
        grid=(batch, nq),
        in_specs=[pl.BlockSpec((tq, MIX_W), lambda b, i: (b * nq + i, 0)),
                  pl.BlockSpec((tq, LANES), lambda b, i: (b * nq + i, C_NG // LANES)),
                  pl.BlockSpec((seq, KV_W), lambda b, i: (b, 0)),
                  pl.BlockSpec((seq, KV_W), lambda b, i: (b, 1)),
                  pl.BlockSpec((seq, 4 * KV_W), lambda b, i: (b, 0)),
                  full(pe), full(w1), full(w2), full(tabs[0]), full(tabs[1]), full(tabs[2]), full(expand)],
        out_specs=pl.BlockSpec((tq, MIX_W), lambda b, i: (b * nq + i, 0)),
        out_shape=jax.ShapeDtypeStruct((m, MIX_W), F32),
        scratch_shapes=[pltpu.VMEM((N_CMP_SLOTS, KV_W), BF16), pltpu.VMEM((N_CMP_SLOTS, KV_W), BF16),
                        pltpu.VMEM((LANES, tq), F32), pltpu.VMEM((seq // SLC_BLOCK, tq), F32),
                        pltpu.VMEM((NSA_KV_HEADS, tq, LANES), BF16), stat, stat, stat],
        compiler_params=_cparams(("parallel", "arbitrary")),
        name="nsa_prompt",
    )(q, p, cmp_rows, cmp_rows, kvb, pe, w1, w2, *tabs, expand)


def _nsa_sample_kernel(*refs, past, dec, n_pages, n_seq):
    n_pg = n_seq * n_pages
    cpages, spages = refs[1:1 + n_pg], refs[1 + n_pg:1 + 2 * n_pg]
    (wbuf_ref, q_ref, ng_ref, slc_ref, win_ref, pe_ref, w1_ref, w2_ref, a_ref, bm_ref, bp_ref, exp_ref,
     o_ref, wout_ref, kcr_s, vcr_s, kslc_s, vslc_s) = refs[1 + 2 * n_pg:]
    for sb in range(n_seq):
        rows = slice(sb * dec, (sb + 1) * dec)
        pgs = slice(sb * n_pages, (sb + 1) * n_pages)
        _nsa_sample_one(cpages[pgs], spages[pgs], wbuf_ref.at[sb], q_ref.at[rows], ng_ref.at[rows], slc_ref.at[rows],
                        win_ref.at[rows], pe_ref, w1_ref, w2_ref, a_ref, bm_ref, bp_ref, exp_ref, o_ref.at[rows],
                        wout_ref.at[sb], kcr_s.at[sb], vcr_s.at[sb], kslc_s.at[sb], vslc_s.at[sb], past=past, dec=dec)


def _nsa_sample_one(cpages, spages, wbuf_ref, q_ref, ng_ref, slc_ref, win_ref, pe_ref, w1_ref, w2_ref, a_ref, bm_ref,
                    bp_ref, exp_ref, o_ref, wout_ref, kcr_s, vcr_s, kslc_s, vslc_s, *, past, dec):
    n_pages = len(spages)
    chunks = PAGE_SIZE // CMP_STRIDE
    for pg in range(n_pages):
        for c in range(chunks):
            src = slice(c * CMP_STRIDE, (c + 1) * CMP_STRIDE)
            dst = slice((pg * chunks + c) * CMP_PITCH, (pg * chunks + c) * CMP_PITCH + CMP_STRIDE)
            kcr_s[dst, :] = cpages[pg][0, src, :KV_W]
            vcr_s[dst, :] = cpages[pg][0, src, KV_W:]
    for pg in range(n_pages):
        r0 = pg * PAGE_SIZE
        kslc_s[r0:r0 + PAGE_SIZE, :] = spages[pg][0, :, :KV_W].astype(BF16)
        vslc_s[r0:r0 + PAGE_SIZE, :] = spages[pg][0, :, KV_W:].astype(BF16)

    n_rows = NSA_HEADS * dec
    tail = LANES - dec
    new_slc = jnp.concatenate([slc_ref[...], jnp.zeros((tail, 2 * KV_W), F32)], axis=0).astype(BF16)
    kslc_s[past:past + LANES, :] = new_slc[:, :KV_W]
    vslc_s[past:past + LANES, :] = new_slc[:, KV_W:]
    k_cmp, v_cmp = _compress(_strided_rows(kcr_s, CMP_PITCH), _strided_rows(vcr_s, CMP_PITCH), pe_ref, w1_ref, w2_ref,
                             a_ref, bm_ref, bp_ref)

    lane8 = lax.broadcasted_iota(jnp.int32, (dec, LANES), 1)
    lane = lax.broadcasted_iota(jnp.int32, (n_rows, LANES), 1)
    rowi = lax.broadcasted_iota(jnp.int32, (n_rows, LANES), 0)
    qpos = past + rowi % dec
    qs = jnp.concatenate([_group_query(q_ref, h, lane8) for h in range(NSA_HEADS)], axis=0).astype(BF16)

    cmask = (CMP_STRIDE * lane + CMP_BLOCK - 1) <= qpos
    p = _masked_softmax(_dot_nt(qs, k_cmp.astype(BF16)) * ATT_SCALE, cmask)
    o_cmp = _dot(p.astype(BF16), v_cmp.astype(BF16))

    sels = []
    qpos8 = past + lax.broadcasted_iota(jnp.int32, (dec, LANES), 0)
    for grp in range(NSA_KV_HEADS):
        base = grp * NSA_GROUP * dec
        psum = p[base:base + dec]
        for r in range(1, NSA_GROUP):
            psum = psum + p[base + r * dec:base + (r + 1) * dec]
        score, allowed = _block_scores(psum, qpos8, lane8)
        cnt = _rank_count(score, lane8, jnp.where(score < FORCE_SCORE, 1.0, 0.0))
        sel = jnp.where(allowed & (cnt < TOP_N), 1.0, 0.0)
        sels += [sel] * NSA_GROUP
    sel = jnp.concatenate(sels, axis=0).astype(BF16)
    n_keys = past + LANES
    kpos = lax.broadcasted_iota(jnp.int32, (n_rows, n_keys), 1)
    qpos_k = past + lax.broadcasted_iota(jnp.int32, (n_rows, n_keys), 0) % dec
    kmask = ((_dot(sel, exp_ref[...]) > 0.5) | (kpos >= past)) & (kpos <= qpos_k)
    p = _masked_softmax(_dot_nt(qs, kslc_s[...]) * ATT_SCALE, kmask)
    o_slc = _dot(p.astype(BF16), vslc_s[...])

    wb = wbuf_ref.shape[0]
    buf = wbuf_ref[...]
    new = win_ref[...]
    zpad = jnp.zeros((tail, KV_W), F32)
    k_win = jnp.concatenate([buf[:, :KV_W], new[:, :KV_W], zpad], axis=0).astype(BF16)
    v_win = jnp.concatenate([buf[:, KV_W:], new[:, KV_W:], zpad], axis=0).astype(BF16)
    n_w = wb + LANES
    kidx = lax.broadcasted_iota(jnp.int32, (n_rows, n_w), 1)
    kpos_w = past - wb + kidx
    d = past + lax.broadcasted_iota(jnp.int32, (n_rows, n_w), 0) % dec - kpos_w
    wmask = (d >= 0) & (d <= WINDOW) & (kpos_w >= 0)
    p = _masked_softmax(_dot_nt(qs, k_win) * ATT_SCALE, wmask)
    o_win = _dot(p.astype(BF16), v_win)
    wout_ref[:wb - dec, :] = buf[dec:, :]
    wout_ref[wb - dec:, :] = new

    g = _sigmoid(ng_ref[...])
    left = lane8 < NSA_HD
    for pair in range(NSA_HEADS // 2):
        tile = jnp.zeros((dec, LANES), F32)
        for br, o_b in enumerate((o_cmp, o_slc, o_win)):
            pa = _place(o_b[(2 * pair) * dec:(2 * pair + 1) * dec], 2 * pair, lane8)
            pb = _place(o_b[(2 * pair + 1) * dec:(2 * pair + 2) * dec], 2 * pair + 1, lane8)
            tile = tile + _gate_tile(g, br, pair, left) * jnp.where(left, pa, pb)
        o_ref[:, pair * LANES:(pair + 1) * LANES] = tile


def _nsa_sample(q, p, slc_new, win_new, pool_cmp, pool_slc, win_buf, page_table, layer, cw, batch, dec, n_seq):
    m = q.shape[0]
    n_pages = page_table.shape[1]
    past = n_pages * PAGE_SIZE
    n_pool = pool_cmp.shape[0] // 4
    wb = win_buf.shape[1]
    assert past == CMP_STRIDE * N_CMP_SLOTS and past % SLC_BLOCK == 0 and dec % 8 == 0 and dec <= SLC_BLOCK
    assert (past + dec - CMP_BLOCK) // CMP_STRIDE + 1 == N_CMP_SLOTS - 1 and wb % 8 == 0
    pe, w1, w2, tabs = cw
    full = lambda a: pl.BlockSpec(a.shape, lambda b, pt, nd=a.ndim: (0,) * nd)
    expand = _select_expand(past + LANES)
    assert batch % n_seq == 0
    n_pg = n_seq * n_pages
    pages = [pl.BlockSpec((1, PAGE_SIZE, 2 * KV_W), lambda b, pt, j=j: (layer * n_pool + pt[b * n_pg + j], 0, 0))
             for j in range(n_pg)]
    rows = lambda w, c=0: pl.BlockSpec((n_seq * dec, w), lambda b, pt, c=c: (b, c))
    grid_spec = pltpu.PrefetchScalarGridSpec(
        num_scalar_prefetch=1,
        grid=(batch // n_seq,),
        in_specs=pages + pages
        + [pl.BlockSpec((n_seq, wb, 2 * KV_W), lambda b, pt: (layer * (batch // n_seq) + b, 0, 0)),
           rows(MIX_W), rows(LANES, C_NG // LANES), rows(2 * KV_W), rows(2 * KV_W),
           full(pe), full(w1), full(w2), full(tabs[0]), full(tabs[1]), full(tabs[2]), full(expand)],
        out_specs=[rows(MIX_W), pl.BlockSpec((n_seq, wb, 2 * KV_W), lambda b, pt: (b, 0, 0))],
        scratch_shapes=[pltpu.VMEM((n_seq, N_CMP_SLOTS * CMP_PITCH, KV_W), F32),
                        pltpu.VMEM((n_seq, N_CMP_SLOTS * CMP_PITCH, KV_W), F32),
                        pltpu.VMEM((n_seq, past + LANES, KV_W), BF16), pltpu.VMEM((n_seq, past + LANES, KV_W), BF16)],
    )
    return pl.pallas_call(
        functools.partial(_nsa_sample_kernel, past=past, dec=dec, n_pages=n_pages, n_seq=n_seq),
        grid_spec=grid_spec,
        out_shape=[jax.ShapeDtypeStruct((m, MIX_W), F32), jax.ShapeDtypeStruct((batch, wb, 2 * KV_W), F32)],
        compiler_params=_cparams(("parallel",)),
        name="nsa_sample",
    )(page_table.reshape(-1), *([pool_cmp] * n_pg), *([pool_slc] * n_pg), win_buf, q, p, slc_new, win_new,
      pe, w1, w2, *tabs, expand)


def _merge_kernel(x_ref, hg_ref, nsa_ref, ga_ref, gb_ref, wa_ref, wb_ref, wo_ref, o_ref):
    a = _dot(hg_ref[...].astype(BF16), wa_ref[...])
    b = _dot(nsa_ref[...].astype(BF16), wb_ref[...])
    merged = _sigmoid(ga_ref[...]) * a + _sigmoid(gb_ref[...]) * b
    o_ref[...] = x_ref[...] + _dot(merged.astype(BF16), wo_ref[...])


def _merge(x, o_hg, o_nsa, p, w_a, w_b, w_o, tm):
    m = x.shape[0]
    full = lambda a: pl.BlockSpec(a.shape, lambda i: (0, 0))
    return pl.pallas_call(
        _merge_kernel,
        grid=(m // tm,),
        in_specs=[pl.BlockSpec((tm, D_MODEL), lambda i: (i, 0)),
                  pl.BlockSpec((tm, MIX_W), lambda i: (i, 0)),
                  pl.BlockSpec((tm, MIX_W), lambda i: (i, 0)),
                  pl.BlockSpec((tm, D_MODEL), lambda i: (i, C_MG // D_MODEL)),
                  pl.BlockSpec((tm, D_MODEL), lambda i: (i, C_MG // D_MODEL + 1)),
                  full(w_a), full(w_b), full(w_o)],
        out_specs=pl.BlockSpec((tm, D_MODEL), lambda i: (i, 0)),
        out_shape=jax.ShapeDtypeStruct((m, D_MODEL), F32),
        compiler_params=_cparams(("parallel",)),
        name="merge",
    )(x, o_hg, o_nsa, p, p, w_a, w_b, w_o)


def _ffn_kernel(*refs, tm, seq, has_state):
    if has_state:
        x_ref, gn_ref, wg_ref, wu_ref, cw_ref, cb_ref, wd_ref, p1_ref, p2_ref, o_ref, gt_ref, xn_ref, acc_ref = refs
    else:
        x_ref, gn_ref, wg_ref, wu_ref, cw_ref, cb_ref, wd_ref, o_ref, gt_ref, xn_ref, acc_ref, carry_ref = refs
    i, f = pl.program_id(0), pl.program_id(1)

    @pl.when(f == 0)
    def _():
        x = x_ref[...]
        ms = jnp.mean(x * x, axis=-1, keepdims=True)
        xn_ref[...] = (x * lax.rsqrt(ms + NORM_EPS) * gn_ref[...]).astype(BF16)
        acc_ref[...] = jnp.zeros_like(acc_ref)

    xn = xn_ref[...]
    g = _dot(xn, wg_ref[...])
    u = _dot(xn, wu_ref[...])
    row = lax.broadcasted_iota(jnp.int32, g.shape, 0)
    g1 = pltpu.roll(g, 1, 0)
    g2 = pltpu.roll(g, 2, 0)
    if has_state:
        tpos = row % seq
        g1 = jnp.where(tpos == 0, p1_ref[...], g1)
        g2 = jnp.where(tpos < 2, p2_ref[...], g2)
        gt_ref[...] = g
    else:
        first = (i % (seq // tm)) == 0
        prev = jnp.where(first, 0.0, carry_ref[f])
        g1 = jnp.where(row == 0, prev[7:8, :], g1)
        g2 = jnp.where(row == 0, prev[6:7, :], jnp.where(row == 1, prev[7:8, :], g2))
        carry_ref[f] = g[tm - 8:, :]
        gt_ref[0] = g[tm - 8:, :]
    conv = cw_ref[0:1, :] * g2 + cw_ref[1:2, :] * g1 + cw_ref[2:3, :] * g + cb_ref[...]
    acc_ref[...] += _dot((_gelu_tanh(conv) * u).astype(BF16), wd_ref[...])

    @pl.when(f == pl.num_programs(1) - 1)
    def _():
        o_ref[...] = x_ref[...] + acc_ref[...]


def _ffn(x, gn, w_gu, conv_w, conv_b, w_d, prev, seq, tm, tf):
    m = x.shape[0]
    nf = D_FF // tf
    has_state = prev is not None
    in_specs = [pl.BlockSpec((tm, D_MODEL), lambda i, f: (i, 0)),
                pl.BlockSpec((1, D_MODEL), lambda i, f: (0, 0)),
                pl.BlockSpec((D_MODEL, tf), lambda i, f: (0, f)),
                pl.BlockSpec((D_MODEL, tf), lambda i, f: (0, nf + f)),
                pl.BlockSpec((CONV_W, tf), lambda i, f: (0, f)),
                pl.BlockSpec((1, tf), lambda i, f: (0, f)),
                pl.BlockSpec((tf, D_MODEL), lambda i, f: (f, 0))]
    args = [x, gn, w_gu, w_gu, conv_w, conv_b, w_d]
    scratch = [pltpu.VMEM((tm, D_MODEL), BF16), pltpu.VMEM((tm, D_MODEL), F32)]
    if has_state:
        assert tm % seq == 0
        in_specs += [pl.BlockSpec((tm, tf), lambda i, f: (i, f))] * 2
        args += list(prev)
        gt_spec = pl.BlockSpec((tm, tf), lambda i, f: (i, f))
        gt_shape = jax.ShapeDtypeStruct((m, D_FF), F32)
    else:
        assert seq % tm == 0
        gt_spec = pl.BlockSpec((1, 8, tf), lambda i, f: (i, 0, f))
        gt_shape = jax.ShapeDtypeStruct((m // tm, 8, D_FF), F32)
        scratch.append(pltpu.VMEM((nf, 8, tf), F32))
    return pl.pallas_call(
        functools.partial(_ffn_kernel, tm=tm, seq=seq, has_state=has_state),
        grid=(m // tm, nf),
        in_specs=in_specs,
        out_specs=[pl.BlockSpec((tm, D_MODEL), lambda i, f: (i, 0)), gt_spec],
        out_shape=[jax.ShapeDtypeStruct((m, D_MODEL), F32), gt_shape],
        scratch_shapes=scratch,
        compiler_params=_cparams(("arbitrary", "arbitrary")),
        name="ffn",
    )(*args)


def _prepare_weights(w_in, hg_lower, cmp_pe, cmp_w1, cmp_w2, w_branch, w_out, w_gate_up, w_down):
    depth = w_in.shape[0]
    c_rest = 4 * MIX_W
    c_ng = c_rest + MIX_W + 6 * KV_W
    n_gate = 3 * NSA_HEADS
    w_in_p = jnp.concatenate([
        w_in[:, :, :c_rest], w_in[:, :, c_ng + n_gate:], w_in[:, :, c_rest:c_ng], w_in[:, :, c_ng:c_ng + n_gate],
        jnp.zeros((depth, D_MODEL, N_IN - w_in.shape[2]), w_in.dtype)], axis=2).astype(BF16)
    lbs = jax.nn.softmax(hg_lower.astype(F32), axis=0)
    lbs = (jnp.cumsum(lbs, axis=0) - lbs[0:1]).reshape(depth, HG_HEADS, 1, HG_DK)
    def block_diag(w):
        w = w.astype(BF16)
        z = jnp.zeros_like(w)
        return jnp.concatenate([jnp.concatenate([w, z], axis=-1), jnp.concatenate([z, w], axis=-1)], axis=-2)

    pe = jnp.concatenate([cmp_pe] * NSA_KV_HEADS, axis=-1)
    return (w_in_p, lbs, pe, block_diag(cmp_w1), block_diag(cmp_w2), w_branch.astype(BF16), w_out.astype(BF16),
            w_gate_up.astype(BF16), w_down.astype(BF16))


def _trunk(x, pos, caches, page_table, weights, batch, seq, cfg):
    (norm_mix, w_in_p, lbs, hg_norm, pe, w1, w2, w_br, w_out, norm_ffn, w_gu, conv_w, conv_b, w_down, norm_final) = weights
    depth = w_in_p.shape[0]
    tm = cfg["tm"]
    tables = _rope_tables(pos)
    if seq < tm:
        tables = tuple(jnp.tile(t, (tm // seq, 1)) for t in tables)
    cmp_tabs = _rope_tables(jnp.arange(N_CMP_SLOTS, dtype=jnp.int32) * CMP_STRIDE + CMP_BLOCK - 1)
    outs = [[] for _ in range(5)]
    for l in range(depth):
        p = _norm_matmul(x, norm_mix[l][None], w_in_p[l], cfg["nm_tm"], cfg["tn"])
        q_rot, cmp_rows, slc_rows, win_rows, kvb = _prep(p, tables, tm)
        s0 = None if caches is None else caches[3][l]
        o_hg, s_hg = _gla(p, lbs[l], hg_norm[l][None], s0, batch, seq, cfg["gla_rows"], cfg["chunk"], cfg["sub"],
                          cfg["gla_mm"], cfg["gla_heads"])
        cw = (pe[l], w1[l], w2[l], cmp_tabs)
        if caches is None:
            o_nsa = _nsa_prompt(q_rot, p, cmp_rows, kvb, cw, batch, seq, LANES)
            new_win = win_rows.reshape(batch, seq, 2 * KV_W)[:, -min(WINDOW, seq):]
        else:
            o_nsa, new_win = _nsa_sample(q_rot, p, slc_rows, win_rows, caches[0], caches[1], caches[2], page_table, l,
                                         cw, batch, seq, cfg["nsa_seqs"])
        x = _merge(x, o_hg, o_nsa, p, w_br[l, 0], w_br[l, 1], w_out[l], tm)
        if caches is None:
            x, g_tail = _ffn(x, norm_ffn[l][None], w_gu[l], conv_w[l], conv_b[l][None], w_down[l], None, seq,
                             cfg["ffn_tm"], cfg["tf"])
            per = seq // cfg["ffn_tm"]
            c_st = g_tail[per - 1::per, 8 - (CONV_W - 1):, :]
        else:
            st = caches[4][l]
            prev2 = jnp.pad(st, ((0, 0), (0, seq - (CONV_W - 1)), (0, 0))).reshape(batch * seq, D_FF)
            prev1 = jnp.pad(st[:, 1:], ((0, 0), (0, seq - 1), (0, 0))).reshape(batch * seq, D_FF)
            x, g_all = _ffn(x, norm_ffn[l][None], w_gu[l], conv_w[l], conv_b[l][None], w_down[l], (prev1, prev2), seq,
                            cfg["ffn_tm"], cfg["tf"])
            c_st = g_all.reshape(batch, seq, D_FF)[:, seq - (CONV_W - 1):]
        kv6 = lambda a: a.reshape(batch, -1, 2, NSA_KV_HEADS, NSA_HD)
        outs[0].append(kv6(cmp_rows))
        outs[1].append(kv6(slc_rows))
        outs[2].append(kv6(new_win))
        outs[3].append(s_hg)
        outs[4].append(c_st)
    y = _rmsnorm(x, norm_final[None], tm).reshape(batch, seq, D_MODEL)
    return (y,) + tuple(jnp.stack(o) for o in outs)


def kernel(x_prompt, x_sample, cache_cmp_kv, cache_slc_kv, cache_win_kv, state_hgrn, state_conv, page_table, norm_mix, w_in, hg_lower, hg_norm, cmp_pe, cmp_w1, cmp_w2, w_branch, w_out, norm_ffn, w_gate_up, conv_w, conv_b, w_down, norm_final):
    (w_in_p, lbs, pe, w1, w2, w_br, w_o, w_gu, w_d) = _prepare_weights(
        w_in, hg_lower, cmp_pe, cmp_w1, cmp_w2, w_branch, w_out, w_gate_up, w_down)
    weights = (norm_mix, w_in_p, lbs, hg_norm, pe, w1, w2, w_br, w_o, norm_ffn, w_gu, conv_w, conv_b, w_d, norm_final)
    b_p, l_p, _ = x_prompt.shape
    b_s, l_s, _ = x_sample.shape
    depth, n_pool = cache_cmp_kv.shape[:2]
    past = page_table.shape[1] * PAGE_SIZE
    pos_p = jnp.arange(l_p, dtype=jnp.int32)
    pos_s = past + jnp.arange(l_s, dtype=jnp.int32)
    cfg_p = dict(tm=1024, nm_tm=1024, tn=1408, gla_rows=256, chunk=64, sub=16, gla_mm=BF16, gla_heads=2, ffn_tm=512,
                 tf=1408)
    cfg_s = dict(tm=256, nm_tm=1024, tn=1408, gla_rows=l_s, chunk=l_s, sub=l_s, gla_mm=F32, gla_heads=HG_HEADS,
                 nsa_seqs=2, ffn_tm=256, tf=1408)
    res_p = _trunk(x_prompt.reshape(b_p * l_p, D_MODEL), pos_p, None, None, weights, b_p, l_p, cfg_p)
    caches = (cache_cmp_kv.reshape(depth * n_pool, PAGE_SIZE, 2 * KV_W),
              cache_slc_kv.reshape(depth * n_pool, PAGE_SIZE, 2 * KV_W),
              cache_win_kv.reshape(depth * b_s, -1, 2 * KV_W), state_hgrn, state_conv)
    res_s = _trunk(x_sample.reshape(b_s * l_s, D_MODEL), pos_s, caches, page_table, weights, b_s, l_s, cfg_s)
    out = []
    for a, b in zip(res_p, res_s):
        out += [a, b]
    out[7] = out[7].reshape(depth, b_s, -1, 2, NSA_KV_HEADS, NSA_HD)
    return tuple(out)
```

```python
import functools

import jax
import jax.numpy as jnp
from jax import lax
from jax.experimental import pallas as pl
from jax.experimental.pallas import tpu as pltpu

F32 = jnp.float32
BF16 = jnp.bfloat16

D_MODEL = 1024
MIX_W = 512
HG_DK = 128
HG_HEADS = 4
HG_DV = 128
NSA_HD = 64
NSA_HEADS = 8
NSA_KV_HEADS = 2
NSA_GROUP = 4
KV_W = NSA_KV_HEADS * NSA_HD
CMP_BLOCK = 32
CMP_STRIDE = 16
SLC_BLOCK = 64
TOP_N = 16
WINDOW = 512
ROPE_THETA = 500000.0
ROPE_DIM = 16
D_FF = 2816
CONV_W = 3
PAGE_SIZE = 128
NORM_EPS = 1e-6
NEG_INF = -1e30
LB_FLOOR = 1e-30
FORCE_SCORE = 1e9
ATT_SCALE = NSA_HD ** -0.5
LOG2E = 1.4426950408889634

LANES = 128
N_CMP_SLOTS = 128
SLC_CHUNK = 512
CMP_PITCH = 24
VMEM_LIMIT = 56 * 1024 * 1024

C_HQ, C_HF, C_HI, C_HG = 0, 512, 1024, 1536
C_MG = 2048
C_NQ = 4096
C_KV = 4608
C_NG = 5376
N_IN = 5632


def _cparams(sem):
    return pltpu.CompilerParams(dimension_semantics=sem, vmem_limit_bytes=VMEM_LIMIT)


def _sigmoid(x):
    return 1.0 / (1.0 + jnp.exp(-x))


def _gelu_tanh(x):
    return 0.5 * x * (1.0 + jnp.tanh(0.7978845608028654 * (x + 0.044715 * (x * x * x))))


def _dot(a, b):
    return jnp.dot(a, b, preferred_element_type=F32)


def _dot_nt(a, b):
    return lax.dot_general(a, b, (((1,), (1,)), ((), ())), preferred_element_type=F32)


def _masked_softmax(s, mask):
    s = jnp.where(mask, s, NEG_INF)
    m = jnp.max(s, axis=-1, keepdims=True)
    e = jnp.where(mask, jnp.exp(s - m), 0.0)
    return e * (1.0 / jnp.maximum(jnp.sum(e, axis=-1, keepdims=True), 1e-30))


def _norm_matmul_kernel(x_ref, g_ref, w_ref, o_ref, xn_ref):
    @pl.when(pl.program_id(1) == 0)
    def _():
        x = x_ref[...]
        ms = jnp.mean(x * x, axis=-1, keepdims=True)
        xn_ref[...] = (x * lax.rsqrt(ms + NORM_EPS) * g_ref[...]).astype(BF16)

    o_ref[...] = _dot(xn_ref[...], w_ref[...])


def _norm_matmul(x, g, w, tm, tn):
    m, k = x.shape
    n = w.shape[1]
    return pl.pallas_call(
        _norm_matmul_kernel,
        grid=(m // tm, n // tn),
        in_specs=[pl.BlockSpec((tm, k), lambda i, j: (i, 0)),
                  pl.BlockSpec((1, k), lambda i, j: (0, 0)),
                  pl.BlockSpec((k, tn), lambda i, j: (0, j))],
        out_specs=pl.BlockSpec((tm, tn), lambda i, j: (i, j)),
        out_shape=jax.ShapeDtypeStruct((m, n), F32),
        scratch_shapes=[pltpu.VMEM((tm, k), BF16)],
        compiler_params=_cparams(("parallel", "arbitrary")),
        name="norm_matmul",
    )(x, g, w)


def _rmsnorm_kernel(x_ref, g_ref, o_ref):
    x = x_ref[...]
    ms = jnp.mean(x * x, axis=-1, keepdims=True)
    o_ref[...] = x * lax.rsqrt(ms + NORM_EPS) * g_ref[...]


def _rmsnorm(x, g, tm):
    m, k = x.shape
    return pl.pallas_call(
        _rmsnorm_kernel,
        grid=(m // tm,),
        in_specs=[pl.BlockSpec((tm, k), lambda i: (i, 0)), pl.BlockSpec((1, k), lambda i: (0, 0))],
        out_specs=pl.BlockSpec((tm, k), lambda i: (i, 0)),
        out_shape=jax.ShapeDtypeStruct((m, k), F32),
        compiler_params=_cparams(("parallel",)),
        name="final_norm",
    )(x, g)


def _rope_tables(pos):
    half = ROPE_DIM // 2
    inv = jnp.float32(ROPE_THETA) ** (-jnp.arange(half, dtype=F32) * 2.0 / ROPE_DIM)
    ang = pos.astype(F32)[:, None] * inv[None, :]
    cos, sin = jnp.cos(ang), jnp.sin(ang)
    t = pos.shape[0]
    z8 = jnp.zeros((t, half), F32)
    rest = NSA_HD - ROPE_DIM
    a = jnp.concatenate([cos, cos, jnp.ones((t, rest), F32)], axis=1)
    bm = jnp.concatenate([-sin, z8, jnp.zeros((t, rest), F32)], axis=1)
    bp = jnp.concatenate([z8, sin, jnp.zeros((t, rest), F32)], axis=1)
    return tuple(jnp.concatenate([v, v], axis=1) for v in (a, bm, bp))


def _rope(x, a, bm, bp):
    half = ROPE_DIM // 2
    return x * a + pltpu.roll(x, LANES - half, 1) * bm + pltpu.roll(x, half, 1) * bp


def _prep_kernel(nq_ref, kc_ref, vc_ref, ks_ref, vs_ref, kw_ref, vw_ref, a_ref, bm_ref, bp_ref,
                 q_ref, cmp_ref, slc_ref, win_ref, kvb_ref):
    a, bm, bp = a_ref[...], bm_ref[...], bp_ref[...]
    for c in range(MIX_W // LANES):
        q_ref[:, c * LANES:(c + 1) * LANES] = _rope(nq_ref[:, c * LANES:(c + 1) * LANES], a, bm, bp)
    cmp_ref[:, :KV_W] = kc_ref[...]
    cmp_ref[:, KV_W:] = vc_ref[...]
    ks = _rope(ks_ref[...], a, bm, bp)
    vs = vs_ref[...]
    slc_ref[:, :KV_W] = ks
    slc_ref[:, KV_W:] = vs
    kw = _rope(kw_ref[...], a, bm, bp)
    vw = vw_ref[...]
    win_ref[:, :KV_W] = kw
    win_ref[:, KV_W:] = vw
    kvb_ref[:, 0 * KV_W:1 * KV_W] = ks.astype(BF16)
    kvb_ref[:, 1 * KV_W:2 * KV_W] = vs.astype(BF16)
    kvb_ref[:, 2 * KV_W:3 * KV_W] = kw.astype(BF16)
    kvb_ref[:, 3 * KV_W:4 * KV_W] = vw.astype(BF16)


def _prep(p, tables, tm):
    m = p.shape[0]
    nt = tables[0].shape[0] // tm
    kvb0 = C_KV // KV_W
    col = lambda c: pl.BlockSpec((tm, KV_W), lambda i, c=c: (i, kvb0 + c))
    tab = pl.BlockSpec((tm, LANES), lambda i: (i % nt, 0))
    row = lambda w: pl.BlockSpec((tm, w), lambda i: (i, 0))
    return pl.pallas_call(
        _prep_kernel,
        grid=(m // tm,),
        in_specs=[pl.BlockSpec((tm, MIX_W), lambda i: (i, C_NQ // MIX_W))] + [col(c) for c in range(6)] + [tab] * 3,
        out_specs=[row(MIX_W), row(2 * KV_W), row(2 * KV_W), row(2 * KV_W), row(4 * KV_W)],
        out_shape=[jax.ShapeDtypeStruct((m, MIX_W), F32)] + [jax.ShapeDtypeStruct((m, 2 * KV_W), F32)] * 3
        + [jax.ShapeDtypeStruct((m, 4 * KV_W), BF16)],
        compiler_params=_cparams(("parallel",)),
        name="prep",
    )(p, p, p, p, p, p, p, *tables)


def _gla_kernel(*refs, chunk, sub, n_chunks, has_s0, mm_dtype, heads):
    if has_s0:
        q_ref, f_ref, i_ref, g_ref, lb_ref, ng_ref, s0_ref, o_ref, so_ref, st_ref, b_s, k_s = refs
    else:
        q_ref, f_ref, i_ref, g_ref, lb_ref, ng_ref, o_ref, so_ref, st_ref, b_s, k_s = refs
        s0_ref = None
    t = pl.program_id(2)

    @pl.when(t == 0)
    def _():
        for hh in range(heads):
            st_ref[hh] = jnp.zeros((HG_DV, HG_DK), F32) if s0_ref is None else s0_ref[0, hh].T

    for hh in range(heads):
        cols = slice(hh * HG_DK, (hh + 1) * HG_DK)
        _gla_head(q_ref.at[:, cols], f_ref.at[:, cols], i_ref.at[:, cols], g_ref.at[:, cols], lb_ref[hh], ng_ref[...],
                  o_ref.at[:, cols], st_ref.at[hh], b_s.at[hh], k_s.at[hh],
                  chunk=chunk, sub=sub, n_chunks=n_chunks, mm_dtype=mm_dtype)

    @pl.when(t == pl.num_programs(2) - 1)
    def _():
        for hh in range(heads):
            so_ref[0, hh] = st_ref[hh].T


def _gla_head(q_ref, f_ref, i_ref, g_ref, lb, ng, o_ref, st_ref, b_all, k_all, *, chunk, sub, n_chunks, mm_dtype):
    log_lb = jnp.log(jnp.maximum(lb, LB_FLOOR))
    log_1m = jnp.log1p(-lb)
    one_m = 1.0 - lb
    row = lax.broadcasted_iota(jnp.int32, (chunk, HG_DK), 0)
    row_s = lax.broadcasted_iota(jnp.int32, (sub, HG_DK), 0)
    lane_s = lax.broadcasted_iota(jnp.int32, (sub, LANES), 1)
    pad = LANES - chunk

    st = st_ref[...]
    for c in range(n_chunks):
        r0 = c * chunk
        b_s, k_s = b_all.at[c], k_all.at[c]
        fr = f_ref[pl.ds(r0, chunk), :]
        qr = q_ref[pl.ds(r0, chunk), :]
        v = i_ref[pl.ds(r0, chunk), :]
        log_sig = jnp.minimum(fr, 0.0) - jnp.log1p(jnp.exp(-jnp.abs(fr)))
        bb = log_1m + log_sig
        log_f = jnp.maximum(log_lb, bb) + jnp.log1p(jnp.exp(-jnp.abs(log_lb - bb)))
        k = one_m / (1.0 + jnp.exp(fr))
        q = qr * _sigmoid(qr) * (HG_DK ** -0.5)
        b = log_f
        sh = 1
        while sh < chunk:
            b = b + jnp.where(row >= sh, pltpu.roll(b, sh, 0), 0.0)
            sh *= 2
        b_s[...] = b
        k_s[...] = k
        o = _dot_nt((q * jnp.exp(b)).astype(mm_dtype), st.astype(mm_dtype))
        pad_rows = lambda a: jnp.concatenate([a, jnp.zeros((pad, HG_DK), F32)], axis=0) if pad > 0 else a
        v_p = pad_rows(v)
        parts = []
        for blk in range(chunk // sub):
            lo = blk * sub
            b_i = b[lo:lo + sub]
            q_i = q[lo:lo + sub]
            if blk > 0:
                ref_b = b_s[lo - 1:lo, :]
                q_d = q_i * jnp.exp(b_i - ref_b)
                k_d = jnp.where(row < lo, k * jnp.exp(jnp.minimum(ref_b - b, 0.0)), 0.0)
                att = _dot_nt(q_d.astype(mm_dtype), pad_rows(k_d).astype(mm_dtype))
            else:
                att = jnp.zeros((sub, LANES), F32)
            zs = []
            for s in range(sub):
                e = jnp.exp(jnp.where(row_s >= s, b_i - b_s[lo + s:lo + s + 1, :], NEG_INF))
                zs.append((q_i * e * k_s[lo + s:lo + s + 1, :]).astype(mm_dtype))
            red = _dot(jnp.concatenate(zs, axis=0), jnp.ones((HG_DK, LANES), mm_dtype))
            for s in range(sub):
                att = jnp.where(lane_s == lo + s, red[s * sub:(s + 1) * sub], att)
            parts.append(att)
        att = parts[0] if len(parts) == 1 else jnp.concatenate(parts, axis=0)
        o = o + _dot(att.astype(mm_dtype), v_p.astype(mm_dtype))
        b_last = b_s[chunk - 1:chunk, :]
        k_dec = pad_rows(k * jnp.exp(b_last - b))
        st = jnp.exp(b_last) * st + _dot(v_p.T.astype(mm_dtype), k_dec.astype(mm_dtype))
        gr = g_ref[pl.ds(r0, chunk), :]
        ms = jnp.mean(o * o, axis=-1, keepdims=True)
        o_ref[pl.ds(r0, chunk), :] = o * lax.rsqrt(ms + NORM_EPS) * ng * (gr * _sigmoid(gr))
    st_ref[...] = st


def _gla(p, lb, ng, s0, batch, seq, rows, chunk, sub, mm_dtype, heads):
    m = p.shape[0]
    nt = seq // rows
    has_s0 = s0 is not None
    w = heads * HG_DK
    col = lambda c0: pl.BlockSpec((rows, w), lambda b, h, t, c0=c0: (b * nt + t, c0 // w + h))
    in_specs = [col(C_HQ), col(C_HF), col(C_HI), col(C_HG),
                pl.BlockSpec((heads, 1, HG_DK), lambda b, h, t: (h, 0, 0)),
                pl.BlockSpec((1, HG_DV), lambda b, h, t: (0, 0))]
    args = [p, p, p, p, lb, ng]
    if has_s0:
        in_specs.append(pl.BlockSpec((1, heads, HG_DK, HG_DV), lambda b, h, t: (b, h, 0, 0)))
        args.append(s0)
    return pl.pallas_call(
        functools.partial(_gla_kernel, chunk=chunk, sub=sub, n_chunks=rows // chunk, has_s0=has_s0, mm_dtype=mm_dtype,
                          heads=heads),
        grid=(batch, HG_HEADS // heads, nt),
        in_specs=in_specs,
        out_specs=[pl.BlockSpec((rows, w), lambda b, h, t: (b * nt + t, h)),
                   pl.BlockSpec((1, heads, HG_DK, HG_DV), lambda b, h, t: (b, h, 0, 0))],
        out_shape=[jax.ShapeDtypeStruct((m, MIX_W), F32),
                   jax.ShapeDtypeStruct((batch, HG_HEADS, HG_DK, HG_DV), F32)],
        scratch_shapes=[pltpu.VMEM((heads, HG_DV, HG_DK), F32),
                        pltpu.VMEM((heads, rows // chunk, chunk, HG_DK), F32),
                        pltpu.VMEM((heads, rows // chunk, chunk, HG_DK), F32)],
        compiler_params=_cparams(("parallel", "parallel", "arbitrary")),
        name="gla",
    )(*args)


def _compress_one(row_of_chunks, kv, pe_ref, w1_ref, w2_ref):
    n = N_CMP_SLOTS
    acc_a = jnp.zeros((n, KV_W), F32)
    acc_b = jnp.zeros((n, KV_W), F32)
    for j in range(CMP_STRIDE):
        x = row_of_chunks(j)
        acc_a = acc_a + _dot((x + pe_ref[kv, j:j + 1, :]).astype(BF16), w1_ref[kv, j])
        acc_b = acc_b + _dot((x + pe_ref[kv, CMP_STRIDE + j:CMP_STRIDE + j + 1, :]).astype(BF16),
                             w1_ref[kv, CMP_STRIDE + j])
    h = _gelu_tanh(acc_a + pltpu.roll(acc_b, n - 1, 0))
    out = _dot(h.astype(BF16), w2_ref[kv])
    slot = lax.broadcasted_iota(jnp.int32, (n, KV_W), 0)
    return jnp.where(slot < n - 1, out, 0.0)


def _compress(k_rows, v_rows, pe_ref, w1_ref, w2_ref, a_ref, bm_ref, bp_ref):
    k_cmp = _rope(_compress_one(k_rows, 0, pe_ref, w1_ref, w2_ref), a_ref[...], bm_ref[...], bp_ref[...])
    return k_cmp, _compress_one(v_rows, 1, pe_ref, w1_ref, w2_ref)


def _strided_rows(rows_ref, pitch=CMP_STRIDE):
    return lambda j: rows_ref[pl.ds(j, N_CMP_SLOTS, stride=pitch), :]


def _group_query(q_ref, head, lane):
    grp, pair, half = head // NSA_GROUP, head // 2, head % 2
    x = q_ref[:, pair * LANES:(pair + 1) * LANES]
    if half != grp:
        x = pltpu.roll(x, NSA_HD, 1)
    return jnp.where((lane >= grp * NSA_HD) & (lane < (grp + 1) * NSA_HD), x, 0.0)


def _block_scores(psum, qpos, lane):
    w = psum
    for k in range(1, CMP_BLOCK // CMP_STRIDE + SLC_BLOCK // CMP_STRIDE - 1):
        w = w + jnp.where(lane >= k, pltpu.roll(psum, k, 1), 0.0)
    blk = lane // 4
    cur = qpos // SLC_BLOCK
    forced = (blk == 0) | (blk == cur) | (blk == cur - 1)
    allowed = (blk * SLC_BLOCK <= qpos) & (lane % 4 == 3)
    return jnp.where(allowed, jnp.where(forced, FORCE_SCORE, w), NEG_INF), allowed


def _rank_count(score, lane, cnt):
    for s in range(LANES // 4):
        c = 4 * s + 3
        col = score[:, c:c + 1]
        cnt = cnt + jnp.where((col > score) | ((col == score) & (lane > c)), 1.0, 0.0)
    return cnt


def _place(res, head, lane):
    grp, half = head // NSA_GROUP, head % 2
    return res if half == grp else pltpu.roll(res, NSA_HD, 1)


def _gate_tile(g, branch, pair, left):
    c = branch * NSA_HEADS + 2 * pair
    return jnp.where(left, g[:, c:c + 1], g[:, c + 1:c + 2])


def _nsa_prompt_kernel(q_ref, ng_ref, kcr_ref, vcr_ref, kvb_ref, pe_ref, w1_ref, w2_ref, a_ref, bm_ref, bp_ref, exp_ref,
                       o_ref, kc_s, vc_s, wt_s, sc_s, sel_s, m_s, l_s, acc_s, *, tq, seq):
    i = pl.program_id(1)

    @pl.when(i == 0)
    def _():
        k_cmp, v_cmp = _compress(_strided_rows(kcr_ref), _strided_rows(vcr_ref), pe_ref, w1_ref, w2_ref,
                                 a_ref, bm_ref, bp_ref)
        kc_s[...] = k_cmp.astype(BF16)
        vc_s[...] = v_cmp.astype(BF16)

    q0 = i * tq
    lane = lax.broadcasted_iota(jnp.int32, (tq, LANES), 1)
    left = lane < NSA_HD
    qpos = q0 + lax.broadcasted_iota(jnp.int32, (tq, LANES), 0)
    g = _sigmoid(ng_ref[...])
    qm = [(_group_query(q_ref, h, lane) * (ATT_SCALE * LOG2E)).astype(BF16) for h in range(NSA_HEADS)]
    qg = [jnp.concatenate(qm[grp * NSA_GROUP:(grp + 1) * NSA_GROUP], axis=0) for grp in range(NSA_KV_HEADS)]
    rows_of = lambda a, r: a[r * tq:(r + 1) * tq]
    kc, vc = kc_s[...], vc_s[...]
    cmask = (CMP_STRIDE * lane + CMP_BLOCK - 1) <= qpos
    n_blk = seq // SLC_BLOCK
    blk_t = lax.broadcasted_iota(jnp.int32, (n_blk, tq), 0)
    qpos_t = q0 + lax.broadcasted_iota(jnp.int32, (n_blk, tq), 1)
    cur_t = qpos_t // SLC_BLOCK
    forced_t = (blk_t == 0) | (blk_t == cur_t) | (blk_t == cur_t - 1)
    allowed_t = blk_t * SLC_BLOCK <= qpos_t

    o_cmp, o_slc, o_win = [None] * NSA_HEADS, [None] * NSA_HEADS, [None] * NSA_HEADS
    for grp in range(NSA_KV_HEADS):
        psum = jnp.zeros((tq, LANES), F32)
        s_all = _dot_nt(qg[grp], kc)
        ps = []
        for r in range(NSA_GROUP):
            s = jnp.where(cmask, rows_of(s_all, r), NEG_INF)
            e = jnp.where(cmask, jnp.exp2(s - jnp.max(s, axis=-1, keepdims=True)), 0.0)
            p = e * (1.0 / jnp.maximum(jnp.sum(e, axis=-1, keepdims=True), 1e-30))
            psum = psum + p
            ps.append(p.astype(BF16))
        o_all = _dot(jnp.concatenate(ps, axis=0), vc)
        for r in range(NSA_GROUP):
            o_cmp[grp * NSA_GROUP + r] = _place(rows_of(o_all, r), grp * NSA_GROUP + r, lane)
        w = psum
        for k in range(1, CMP_BLOCK // CMP_STRIDE + SLC_BLOCK // CMP_STRIDE - 1):
            w = w + jnp.where(lane >= k, pltpu.roll(psum, k, 1), 0.0)
        wt_s[...] = w.T
        imp_t = wt_s[pl.ds(SLC_BLOCK // CMP_STRIDE - 1, n_blk, stride=SLC_BLOCK // CMP_STRIDE), :]
        score = jnp.where(allowed_t, jnp.where(forced_t, FORCE_SCORE, imp_t), NEG_INF)
        sc_s[...] = score
        cnt = jnp.zeros((n_blk, tq), F32)
        for c in range(n_blk):
            other = sc_s[c:c + 1, :]
            cnt = cnt + jnp.where((other > score) | ((other == score) & (blk_t > c)), 1.0, 0.0)
        sel_t = jnp.where(allowed_t & (cnt < TOP_N), 1.0, 0.0)
        sel_s[grp] = jnp.concatenate([sel_t, jnp.zeros((LANES - n_blk, tq), F32)], axis=0).T.astype(BF16)

    for h in range(NSA_HEADS):
        m_s[h] = jnp.full((tq, LANES), NEG_INF, F32)
        l_s[h] = jnp.zeros((tq, LANES), F32)
        acc_s[h] = jnp.zeros((tq, LANES), F32)
    kc_w = SLC_CHUNK
    for ck in range(seq // kc_w):
        @pl.when(ck * kc_w <= q0)
        def _(ck=ck):
            kpos = ck * kc_w + lax.broadcasted_iota(jnp.int32, (tq, kc_w), 1)
            causal = kpos <= q0 + lax.broadcasted_iota(jnp.int32, (tq, kc_w), 0)
            k_c = kvb_ref[ck * kc_w:(ck + 1) * kc_w, 0 * KV_W:1 * KV_W]
            v_c = kvb_ref[ck * kc_w:(ck + 1) * kc_w, 1 * KV_W:2 * KV_W]
            for grp in range(NSA_KV_HEADS):
                picked = _dot(sel_s[grp], exp_ref[:, ck * kc_w:(ck + 1) * kc_w]) > 0.5
                bias = jnp.where(picked & causal, 0.0, NEG_INF)
                s_all = _dot_nt(qg[grp], k_c)
                ps, alphas = [], []
                for r in range(NSA_GROUP):
                    h = grp * NSA_GROUP + r
                    s = rows_of(s_all, r) + bias
                    m_prev = m_s[h]
                    m_new = jnp.maximum(m_prev, jnp.max(s, axis=-1, keepdims=True))
                    alpha = jnp.exp2(m_prev - m_new)
                    p = jnp.exp2(s - jnp.concatenate([m_new] * (kc_w // LANES), axis=1))
                    l_s[h] = alpha * l_s[h] + jnp.sum(p, axis=-1, keepdims=True)
                    m_s[h] = m_new
                    ps.append(p.astype(BF16))
                    alphas.append(alpha)
                pv = _dot(jnp.concatenate(ps, axis=0), v_c)
                for r in range(NSA_GROUP):
                    h = grp * NSA_GROUP + r
                    acc_s[h] = alphas[r] * acc_s[h] + rows_of(pv, r)

    n_win = WINDOW + tq
    w0 = pl.multiple_of(jnp.maximum(q0 - WINDOW, 0), tq)
    dwin = (q0 - w0) + lax.broadcasted_iota(jnp.int32, (tq, n_win), 0) - lax.broadcasted_iota(jnp.int32, (tq, n_win), 1)
    wbias = jnp.where((dwin >= 0) & (dwin <= WINDOW), 0.0, NEG_INF)
    k_win = kvb_ref[pl.ds(w0, n_win), 2 * KV_W:3 * KV_W]
    v_win = kvb_ref[pl.ds(w0, n_win), 3 * KV_W:4 * KV_W]
    for grp in range(NSA_KV_HEADS):
        s_all = _dot_nt(qg[grp], k_win)
        ps, inv = [], []
        for r in range(NSA_GROUP):
            s = rows_of(s_all, r) + wbias
            p = jnp.exp2(s - jnp.max(s, axis=-1, keepdims=True))
            inv.append(1.0 / jnp.sum(p, axis=-1, keepdims=True))
            ps.append(p.astype(BF16))
        o_all = _dot(jnp.concatenate(ps, axis=0), v_win)
        for r in range(NSA_GROUP):
            h = grp * NSA_GROUP + r
            o_win[h] = _place(rows_of(o_all, r) * inv[r], h, lane)
            o_slc[h] = _place(acc_s[h] * (1.0 / l_s[h]), h, lane)
    for pair in range(NSA_HEADS // 2):
        ha, hb = 2 * pair, 2 * pair + 1
        o_ref[:, pair * LANES:(pair + 1) * LANES] = (
            _gate_tile(g, 0, pair, left) * jnp.where(left, o_cmp[ha], o_cmp[hb])
            + _gate_tile(g, 1, pair, left) * jnp.where(left, o_slc[ha], o_slc[hb])
            + _gate_tile(g, 2, pair, left) * jnp.where(left, o_win[ha], o_win[hb]))


def _select_expand(n_keys):
    lane = jnp.arange(LANES, dtype=jnp.int32)[:, None]
    key = jnp.arange(n_keys, dtype=jnp.int32)[None, :]
    return ((lane % 4 == 3) & (lane // 4 == key // SLC_BLOCK)).astype(BF16)


def _nsa_prompt(q, p, cmp_rows, kvb, cw, batch, seq, tq):
    m = q.shape[0]
    assert seq == CMP_STRIDE * N_CMP_SLOTS and seq % tq == 0 and tq == LANES
    nq = seq // tq
    pe, w1, w2, tabs = cw
    full = lambda a: pl.BlockSpec(a.shape, lambda b, i, nd=a.ndim: (0,) * nd)
    key_blk = jnp.arange(seq, dtype=jnp.int32)[None, :] // SLC_BLOCK
    expand = (jnp.arange(LANES, dtype=jnp.int32)[:, None] == key_blk).astype(BF16)
    stat = pltpu.VMEM((NSA_HEADS, tq, LANES), F32)
    return pl.pallas_call(
        functools.partial(_nsa_prompt_kernel, tq=tq, seq=seq),
        grid=(batch, nq),
        in_specs=[pl.BlockSpec((tq, MIX_W), lambda b, i: (b * nq + i, 0)),
                  pl.BlockSpec((tq, LANES), lambda b, i: (b * nq + i, C_NG // LANES)),
                  pl.BlockSpec((seq, KV_W), lambda b, i: (b, 0)),
                  pl.BlockSpec((seq, KV_W), lambda b, i: (b, 1)),
                  pl.BlockSpec((seq, 4 * KV_W), lambda b, i: (b, 0)),
                  full(pe), full(w1), full(w2), full(tabs[0]), full(tabs[1]), full(tabs[2]), full(expand)],
        out_specs=pl.BlockSpec((tq, MIX_W), lambda b, i: (b * nq + i, 0)),
        out_shape=jax.ShapeDtypeStruct((m, MIX_W), F32),
        scratch_shapes=[pltpu.VMEM((N_CMP_SLOTS, KV_W), BF16), pltpu.VMEM((N_CMP_SLOTS, KV_W), BF16),
                        pltpu.VMEM((LANES, tq), F32), pltpu.VMEM((seq // SLC_BLOCK, tq), F32),
                        pltpu.VMEM((NSA_KV_HEADS, tq, LANES), BF16), stat, stat, stat],
        compiler_params=_cparams(("parallel", "arbitrary")),
        name="nsa_prompt",
    )(q, p, cmp_rows, cmp_rows, kvb, pe, w1, w2, *tabs, expand)


def _nsa_sample_kernel(*refs, past, dec, n_pages, n_seq):
    n_pg = n_seq * n_pages
    cpages, spages = refs[1:1 + n_pg], refs[1 + n_pg:1 + 2 * n_pg]
    (wbuf_ref, q_ref, ng_ref, slc_ref, win_ref, pe_ref, w1_ref, w2_ref, a_ref, bm_ref, bp_ref, exp_ref,
     o_ref, wout_ref, kcr_s, vcr_s, kslc_s, vslc_s) = refs[1 + 2 * n_pg:]
    for sb in range(n_seq):
        rows = slice(sb * dec, (sb + 1) * dec)
        pgs = slice(sb * n_pages, (sb + 1) * n_pages)
        _nsa_sample_one(cpages[pgs], spages[pgs], wbuf_ref.at[sb], q_ref.at[rows], ng_ref.at[rows], slc_ref.at[rows],
                        win_ref.at[rows], pe_ref, w1_ref, w2_ref, a_ref, bm_ref, bp_ref, exp_ref, o_ref.at[rows],
                        wout_ref.at[sb], kcr_s.at[sb], vcr_s.at[sb], kslc_s.at[sb], vslc_s.at[sb], past=past, dec=dec)


def _nsa_sample_one(cpages, spages, wbuf_ref, q_ref, ng_ref, slc_ref, win_ref, pe_ref, w1_ref, w2_ref, a_ref, bm_ref,
                    bp_ref, exp_ref, o_ref, wout_ref, kcr_s, vcr_s, kslc_s, vslc_s, *, past, dec):
    n_pages = len(spages)
    chunks = PAGE_SIZE // CMP_STRIDE
    for pg in range(n_pages):
        for c in range(chunks):
            src = slice(c * CMP_STRIDE, (c + 1) * CMP_STRIDE)
            dst = slice((pg * chunks + c) * CMP_PITCH, (pg * chunks + c) * CMP_PITCH + CMP_STRIDE)
            kcr_s[dst, :] = cpages[pg][0, src, :KV_W]
            vcr_s[dst, :] = cpages[pg][0, src, KV_W:]
    for pg in range(n_pages):
        r0 = pg * PAGE_SIZE
        kslc_s[r0:r0 + PAGE_SIZE, :] = spages[pg][0, :, :KV_W].astype(BF16)
        vslc_s[r0:r0 + PAGE_SIZE, :] = spages[pg][0, :, KV_W:].astype(BF16)

    n_rows = NSA_HEADS * dec
    tail = LANES - dec
    new_slc = jnp.concatenate([slc_ref[...], jnp.zeros((tail, 2 * KV_W), F32)], axis=0).astype(BF16)
    kslc_s[past:past + LANES, :] = new_slc[:, :KV_W]
    vslc_s[past:past + LANES, :] = new_slc[:, KV_W:]
    k_cmp, v_cmp = _compress(_strided_rows(kcr_s, CMP_PITCH), _strided_rows(vcr_s, CMP_PITCH), pe_ref, w1_ref, w2_ref,
                             a_ref, bm_ref, bp_ref)

    lane8 = lax.broadcasted_iota(jnp.int32, (dec, LANES), 1)
    lane = lax.broadcasted_iota(jnp.int32, (n_rows, LANES), 1)
    rowi = lax.broadcasted_iota(jnp.int32, (n_rows, LANES), 0)
    qpos = past + rowi % dec
    qs = jnp.concatenate([_group_query(q_ref, h, lane8) for h in range(NSA_HEADS)], axis=0).astype(BF16)

    cmask = (CMP_STRIDE * lane + CMP_BLOCK - 1) <= qpos
    p = _masked_softmax(_dot_nt(qs, k_cmp.astype(BF16)) * ATT_SCALE, cmask)
    o_cmp = _dot(p.astype(BF16), v_cmp.astype(BF16))

    sels = []
    qpos8 = past + lax.broadcasted_iota(jnp.int32, (dec, LANES), 0)
    for grp in range(NSA_KV_HEADS):
        base = grp * NSA_GROUP * dec
        psum = p[base:base + dec]
        for r in range(1, NSA_GROUP):
            psum = psum + p[base + r * dec:base + (r + 1) * dec]
        score, allowed = _block_scores(psum, qpos8, lane8)
        cnt = _rank_count(score, lane8, jnp.where(score < FORCE_SCORE, 1.0, 0.0))
        sel = jnp.where(allowed & (cnt < TOP_N), 1.0, 0.0)
        sels += [sel] * NSA_GROUP
    sel = jnp.concatenate(sels, axis=0).astype(BF16)
    n_keys = past + LANES
    kpos = lax.broadcasted_iota(jnp.int32, (n_rows, n_keys), 1)
    qpos_k = past + lax.broadcasted_iota(jnp.int32, (n_rows, n_keys), 0) % dec
    kmask = ((_dot(sel, exp_ref[...]) > 0.5) | (kpos >= past)) & (kpos <= qpos_k)
    p = _masked_softmax(_dot_nt(qs, kslc_s[...]) * ATT_SCALE, kmask)
    o_slc = _dot(p.astype(BF16), vslc_s[...])

    wb = wbuf_ref.shape[0]
    buf = wbuf_ref[...]
    new = win_ref[...]
    zpad = jnp.zeros((tail, KV_W), F32)
    k_win = jnp.concatenate([buf[:, :KV_W], new[:, :KV_W], zpad], axis=0).astype(BF16)
    v_win = jnp.concatenate([buf[:, KV_W:], new[:, KV_W:], zpad], axis=0).astype(BF16)
    n_w = wb + LANES
    kidx = lax.broadcasted_iota(jnp.int32, (n_rows, n_w), 1)
    kpos_w = past - wb + kidx
    d = past + lax.broadcasted_iota(jnp.int32, (n_rows, n_w), 0) % dec - kpos_w
    wmask = (d >= 0) & (d <= WINDOW) & (kpos_w >= 0)
    p = _masked_softmax(_dot_nt(qs, k_win) * ATT_SCALE, wmask)
    o_win = _dot(p.astype(BF16), v_win)
    wout_ref[:wb - dec, :] = buf[dec:, :]
    wout_ref[wb - dec:, :] = new

    g = _sigmoid(ng_ref[...])
    left = lane8 < NSA_HD
    for pair in range(NSA_HEADS // 2):
        tile = jnp.zeros((dec, LANES), F32)
        for br, o_b in enumerate((o_cmp, o_slc, o_win)):
            pa = _place(o_b[(2 * pair) * dec:(2 * pair + 1) * dec], 2 * pair, lane8)
            pb = _place(o_b[(2 * pair + 1) * dec:(2 * pair + 2) * dec], 2 * pair + 1, lane8)
            tile = tile + _gate_tile(g, br, pair, left) * jnp.where(left, pa, pb)
        o_ref[:, pair * LANES:(pair + 1) * LANES] = tile


def _nsa_sample(q, p, slc_new, win_new, pool_cmp, pool_slc, win_buf, page_table, layer, cw, batch, dec, n_seq):
    m = q.shape[0]
    n_pages = page_table.shape[1]
    past = n_pages * PAGE_SIZE
    n_pool = pool_cmp.shape[0] // 4
    wb = win_buf.shape[1]
    assert past == CMP_STRIDE * N_CMP_SLOTS and past % SLC_BLOCK == 0 and dec % 8 == 0 and dec <= SLC_BLOCK
    assert (past + dec - CMP_BLOCK) // CMP_STRIDE + 1 == N_CMP_SLOTS - 1 and wb % 8 == 0
    pe, w1, w2, tabs = cw
    full = lambda a: pl.BlockSpec(a.shape, lambda b, pt, nd=a.ndim: (0,) * nd)
    expand = _select_expand(past + LANES)
    assert batch % n_seq == 0
    n_pg = n_seq * n_pages
    pages = [pl.BlockSpec((1, PAGE_SIZE, 2 * KV_W), lambda b, pt, j=j: (layer * n_pool + pt[b * n_pg + j], 0, 0))
             for j in range(n_pg)]
    rows = lambda w, c=0: pl.BlockSpec((n_seq * dec, w), lambda b, pt, c=c: (b, c))
    grid_spec = pltpu.PrefetchScalarGridSpec(
        num_scalar_prefetch=1,
        grid=(batch // n_seq,),
        in_specs=pages + pages
        + [pl.BlockSpec((n_seq, wb, 2 * KV_W), lambda b, pt: (layer * (batch // n_seq) + b, 0, 0)),
           rows(MIX_W), rows(LANES, C_NG // LANES), rows(2 * KV_W), rows(2 * KV_W),
           full(pe), full(w1), full(w2), full(tabs[0]), full(tabs[1]), full(tabs[2]), full(expand)],
        out_specs=[rows(MIX_W), pl.BlockSpec((n_seq, wb, 2 * KV_W), lambda b, pt: (b, 0, 0))],
        scratch_shapes=[pltpu.VMEM((n_seq, N_CMP_SLOTS * CMP_PITCH, KV_W), F32),
                        pltpu.VMEM((n_seq, N_CMP_SLOTS * CMP_PITCH, KV_W), F32),
                        pltpu.VMEM((n_seq, past + LANES, KV_W), BF16), pltpu.VMEM((n_seq, past + LANES, KV_W), BF16)],
    )
    return pl.pallas_call(
        functools.partial(_nsa_sample_kernel, past=past, dec=dec, n_pages=n_pages, n_seq=n_seq),
        grid_spec=grid_spec,
        out_shape=[jax.ShapeDtypeStruct((m, MIX_W), F32), jax.ShapeDtypeStruct((batch, wb, 2 * KV_W), F32)],
        compiler_params=_cparams(("parallel",)),
        name="nsa_sample",
    )(page_table.reshape(-1), *([pool_cmp] * n_pg), *([pool_slc] * n_pg), win_buf, q, p, slc_new, win_new,
      pe, w1, w2, *tabs, expand)


def _merge_kernel(x_ref, hg_ref, nsa_ref, ga_ref, gb_ref, wa_ref, wb_ref, wo_ref, o_ref):
    a = _dot(hg_ref[...].astype(BF16), wa_ref[...])
    b = _dot(nsa_ref[...].astype(BF16), wb_ref[...])
    merged = _sigmoid(ga_ref[...]) * a + _sigmoid(gb_ref[...]) * b
    o_ref[...] = x_ref[...] + _dot(merged.astype(BF16), wo_ref[...])


def _merge(x, o_hg, o_nsa, p, w_a, w_b, w_o, tm):
    m = x.shape[0]
    full = lambda a: pl.BlockSpec(a.shape, lambda i: (0, 0))
    return pl.pallas_call(
        _merge_kernel,
        grid=(m // tm,),
        in_specs=[pl.BlockSpec((tm, D_MODEL), lambda i: (i, 0)),
                  pl.BlockSpec((tm, MIX_W), lambda i: (i, 0)),
                  pl.BlockSpec((tm, MIX_W), lambda i: (i, 0)),
                  pl.BlockSpec((tm, D_MODEL), lambda i: (i, C_MG // D_MODEL)),
                  pl.BlockSpec((tm, D_MODEL), lambda i: (i, C_MG // D_MODEL + 1)),
                  full(w_a), full(w_b), full(w_o)],
        out_specs=pl.BlockSpec((tm, D_MODEL), lambda i: (i, 0)),
        out_shape=jax.ShapeDtypeStruct((m, D_MODEL), F32),
        compiler_params=_cparams(("parallel",)),
        name="merge",
    )(x, o_hg, o_nsa, p, p, w_a, w_b, w_o)


def _ffn_kernel(*refs, tm, seq, has_state):
    if has_state:
        x_ref, gn_ref, wg_ref, wu_ref, cw_ref, cb_ref, wd_ref, p1_ref, p2_ref, o_ref, gt_ref, xn_ref, acc_ref = refs
    else:
        x_ref, gn_ref, wg_ref, wu_ref, cw_ref, cb_ref, wd_ref, o_ref, gt_ref, xn_ref, acc_ref, carry_ref = refs
    i, f = pl.program_id(0), pl.program_id(1)

    @pl.when(f == 0)
    def _():
        x = x_ref[...]
        ms = jnp.mean(x * x, axis=-1, keepdims=True)
        xn_ref[...] = (x * lax.rsqrt(ms + NORM_EPS) * gn_ref[...]).astype(BF16)
        acc_ref[...] = jnp.zeros_like(acc_ref)

    xn = xn_ref[...]
    g = _dot(xn, wg_ref[...])
    u = _dot(xn, wu_ref[...])
    row = lax.broadcasted_iota(jnp.int32, g.shape, 0)
    g1 = pltpu.roll(g, 1, 0)
    g2 = pltpu.roll(g, 2, 0)
    if has_state:
        tpos = row % seq
        g1 = jnp.where(tpos == 0, p1_ref[...], g1)
        g2 = jnp.where(tpos < 2, p2_ref[...], g2)
        gt_ref[...] = g
    else:
        first = (i % (seq // tm)) == 0
        prev = jnp.where(first, 0.0, carry_ref[f])
        g1 = jnp.where(row == 0, prev[7:8, :], g1)
        g2 = jnp.where(row == 0, prev[6:7, :], jnp.where(row == 1, prev[7:8, :], g2))
        carry_ref[f] = g[tm - 8:, :]
        gt_ref[0] = g[tm - 8:, :]
    conv = cw_ref[0:1, :] * g2 + cw_ref[1:2, :] * g1 + cw_ref[2:3, :] * g + cb_ref[...]
    acc_ref[...] += _dot((_gelu_tanh(conv) * u).astype(BF16), wd_ref[...])

    @pl.when(f == pl.num_programs(1) - 1)
    def _():
        o_ref[...] = x_ref[...] + acc_ref[...]


def _ffn(x, gn, w_gu, conv_w, conv_b, w_d, prev, seq, tm, tf):
    m = x.shape[0]
    nf = D_FF // tf
    has_state = prev is not None
    in_specs = [pl.BlockSpec((tm, D_MODEL), lambda i, f: (i, 0)),
                pl.BlockSpec((1, D_MODEL), lambda i, f: (0, 0)),
                pl.BlockSpec((D_MODEL, tf), lambda i, f: (0, f)),
                pl.BlockSpec((D_MODEL, tf), lambda i, f: (0, nf + f)),
                pl.BlockSpec((CONV_W, tf), lambda i, f: (0, f)),
                pl.BlockSpec((1, tf), lambda i, f: (0, f)),
                pl.BlockSpec((tf, D_MODEL), lambda i, f: (f, 0))]
    args = [x, gn, w_gu, w_gu, conv_w, conv_b, w_d]
    scratch = [pltpu.VMEM((tm, D_MODEL), BF16), pltpu.VMEM((tm, D_MODEL), F32)]
    if has_state:
        assert tm % seq == 0
        in_specs += [pl.BlockSpec((tm, tf), lambda i, f: (i, f))] * 2
        args += list(prev)
        gt_spec = pl.BlockSpec((tm, tf), lambda i, f: (i, f))
        gt_shape = jax.ShapeDtypeStruct((m, D_FF), F32)
    else:
        assert seq % tm == 0
        gt_spec = pl.BlockSpec((1, 8, tf), lambda i, f: (i, 0, f))
        gt_shape = jax.ShapeDtypeStruct((m // tm, 8, D_FF), F32)
        scratch.append(pltpu.VMEM((nf, 8, tf), F32))
    return pl.pallas_call(
        functools.partial(_ffn_kernel, tm=tm, seq=seq, has_state=has_state),
        grid=(m // tm, nf),
        in_specs=in_specs,
        out_specs=[pl.BlockSpec((tm, D_MODEL), lambda i, f: (i, 0)), gt_spec],
        out_shape=[jax.ShapeDtypeStruct((m, D_MODEL), F32), gt_shape],
        scratch_shapes=scratch,
        compiler_params=_cparams(("arbitrary", "arbitrary")),
        name="ffn",
    )(*args)


def _prepare_weights(w_in, hg_lower, cmp_pe, cmp_w1, cmp_w2, w_branch, w_out, w_gate_up, w_down):
    depth = w_in.shape[0]
    c_rest = 4 * MIX_W
    c_ng = c_rest + MIX_W + 6 * KV_W
    n_gate = 3 * NSA_HEADS
    w_in_p = jnp.concatenate([
        w_in[:, :, :c_rest], w_in[:, :, c_ng + n_gate:], w_in[:, :, c_rest:c_ng], w_in[:, :, c_ng:c_ng + n_gate],
        jnp.zeros((depth, D_MODEL, N_IN - w_in.shape[2]), w_in.dtype)], axis=2).astype(BF16)
    lbs = jax.nn.softmax(hg_lower.astype(F32), axis=0)
    lbs = (jnp.cumsum(lbs, axis=0) - lbs[0:1]).reshape(depth, HG_HEADS, 1, HG_DK)
    def block_diag(w):
        w = w.astype(BF16)
        z = jnp.zeros_like(w)
        return jnp.concatenate([jnp.concatenate([w, z], axis=-1), jnp.concatenate([z, w], axis=-1)], axis=-2)

    pe = jnp.concatenate([cmp_pe] * NSA_KV_HEADS, axis=-1)
    return (w_in_p, lbs, pe, block_diag(cmp_w1), block_diag(cmp_w2), w_branch.astype(BF16), w_out.astype(BF16),
            w_gate_up.astype(BF16), w_down.astype(BF16))


def _trunk(x, pos, caches, page_table, weights, batch, seq, cfg):
    (norm_mix, w_in_p, lbs, hg_norm, pe, w1, w2, w_br, w_out, norm_ffn, w_gu, conv_w, conv_b, w_down, norm_final) = weights
    depth = w_in_p.shape[0]
    tm = cfg["tm"]
    tables = _rope_tables(pos)
    if seq < tm:
        tables = tuple(jnp.tile(t, (tm // seq, 1)) for t in tables)
    cmp_tabs = _rope_tables(jnp.arange(N_CMP_SLOTS, dtype=jnp.int32) * CMP_STRIDE + CMP_BLOCK - 1)
    outs = [[] for _ in range(5)]
    for l in range(depth):
        p = _norm_matmul(x, norm_mix[l][None], w_in_p[l], cfg["nm_tm"], cfg["tn"])
        q_rot, cmp_rows, slc_rows, win_rows, kvb = _prep(p, tables, tm)
        s0 = None if caches is None else caches[3][l]
        o_hg, s_hg = _gla(p, lbs[l], hg_norm[l][None], s0, batch, seq, cfg["gla_rows"], cfg["chunk"], cfg["sub"],
                          cfg["gla_mm"], cfg["gla_heads"])
        cw = (pe[l], w1[l], w2[l], cmp_tabs)
        if caches is None:
            o_nsa = _nsa_prompt(q_rot, p, cmp_rows, kvb, cw, batch, seq, LANES)
            new_win = win_rows.reshape(batch, seq, 2 * KV_W)[:, -min(WINDOW, seq):]
        else:
            o_nsa, new_win = _nsa_sample(q_rot, p, slc_rows, win_rows, caches[0], caches[1], caches[2], page_table, l,
                                         cw, batch, seq, cfg["nsa_seqs"])
        x = _merge(x, o_hg, o_nsa, p, w_br[l, 0], w_br[l, 1], w_out[l], tm)
        if caches is None:
            x, g_tail = _ffn(x, norm_ffn[l][None], w_gu[l], conv_w[l], conv_b[l][None], w_down[l], None, seq,
                             cfg["ffn_tm"], cfg["tf"])
            per = seq // cfg["ffn_tm"]
            c_st = g_tail[per - 1::per, 8 - (CONV_W - 1):, :]
        else:
            st = caches[4][l]
            prev2 = jnp.pad(st, ((0, 0), (0, seq - (CONV_W - 1)), (0, 0))).reshape(batch * seq, D_FF)
            prev1 = jnp.pad(st[:, 1:], ((0, 0), (0, seq - 1), (0, 0))).reshape(batch * seq, D_FF)
            x, g_all = _ffn(x, norm_ffn[l][None], w_gu[l], conv_w[l], conv_b[l][None], w_down[l], (prev1, prev2), seq,
                            cfg["ffn_tm"], cfg["tf"])
            c_st = g_all.reshape(batch, seq, D_FF)[:, seq - (CONV_W - 1):]
        kv6 = lambda a: a.reshape(batch, -1, 2, NSA_KV_HEADS, NSA_HD)
        outs[0].append(kv6(cmp_rows))
        outs[1].append(kv6(slc_rows))
        outs[2].append(kv6(new_win))
        outs[3].append(s_hg)
        outs[4].append(c_st)
    y = _rmsnorm(x, norm_final[None], tm).reshape(batch, seq, D_MODEL)
    return (y,) + tuple(jnp.stack(o) for o in outs)


def kernel(x_prompt, x_sample, cache_cmp_kv, cache_slc_kv, cache_win_kv, state_hgrn, state_conv, page_table, norm_mix, w_in, hg_lower, hg_norm, cmp_pe, cmp_w1, cmp_w2, w_branch, w_out, norm_ffn, w_gate_up, conv_w, conv_b, w_down, norm_final):
    (w_in_p, lbs, pe, w1, w2, w_br, w_o, w_gu, w_d) = _prepare_weights(
        w_in, hg_lower, cmp_pe, cmp_w1, cmp_w2, w_branch, w_out, w_gate_up, w_down)
    weights = (norm_mix, w_in_p, lbs, hg_norm, pe, w1, w2, w_br, w_o, norm_ffn, w_gu, conv_w, conv_b, w_d, norm_final)
    b_p, l_p, _ = x_prompt.shape
    b_s, l_s, _ = x_sample.shape
    depth, n_pool = cache_cmp_kv.shape[:2]
    past = page_table.shape[1] * PAGE_SIZE
    pos_p = jnp.arange(l_p, dtype=jnp.int32)
    pos_s = past + jnp.arange(l_s, dtype=jnp.int32)
    cfg_p = dict(tm=1024, nm_tm=1024, tn=1408, gla_rows=256, chunk=64, sub=16, gla_mm=BF16, gla_heads=4, ffn_tm=512,
                 tf=1408)
    cfg_s = dict(tm=256, nm_tm=1024, tn=1408, gla_rows=l_s, chunk=l_s, sub=l_s, gla_mm=F32, gla_heads=HG_HEADS,
                 nsa_seqs=2, ffn_tm=256, tf=1408)
    res_p = _trunk(x_prompt.reshape(b_p * l_p, D_MODEL), pos_p, None, None, weights, b_p, l_p, cfg_p)
    caches = (cache_cmp_kv.reshape(depth * n_pool, PAGE_SIZE, 2 * KV_W),
              cache_slc_kv.reshape(depth * n_pool, PAGE_SIZE, 2 * KV_W),
              cache_win_kv.reshape(depth * b_s, -1, 2 * KV_W), state_hgrn, state_conv)
    res_s = _trunk(x_sample.reshape(b_s * l_s, D_MODEL), pos_s, caches, page_table, weights, b_s, l_s, cfg_s)
    out = []
    for a, b in zip(res_p, res_s):
        out += [a, b]
    out[7] = out[7].reshape(depth, b_s, -1, 2, NSA_KV_HEADS, NSA_HD)
    return tuple(out)
```
